```python
import jax, jax.numpy as jnp
from jax import lax
import numpy as np

D_MODEL = 1024
BATCH = 1
SEQ = 16384
DEPTH = 4

CHUNK = 64
MEM_LEN = 256
D_MIX = D_MODEL
HEAD_DIM = 64
SSM_WIDTH = D_MIX // 2
SSM_HEADS = SSM_WIDTH // HEAD_DIM
SSM_GROUPS = 2
SSM_STATE = 128
CONV_WIDTH = 4
CONV_DIM = SSM_WIDTH + 2 * SSM_GROUPS * SSM_STATE
SSM_IN = CONV_DIM + SSM_WIDTH + SSM_HEADS
RWKV_WIDTH = D_MIX // 4
RWKV_HEADS = RWKV_WIDTH // HEAD_DIM
DECAY_LORA = 64
AAA_LORA = 64
RWKV_IN = 4 * RWKV_WIDTH + DECAY_LORA + AAA_LORA
XATTN_WIDTH = D_MIX - SSM_WIDTH - RWKV_WIDTH
XATTN_HEADS = XATTN_WIDTH // HEAD_DIM
XATTN_IN = 2 * XATTN_WIDTH
IN_WIDTH = SSM_IN + RWKV_IN + XATTN_IN
NORM_EPS = 1e-6
LNX_EPS = 64e-5
L2_EPS = 1e-12

kernel_name = "hymba_ssd_rwkv7_memxattn_trunk"


def rmsnorm(x, w):
    xf = x.astype(jnp.float32)
    y = xf * lax.rsqrt(jnp.mean(xf * xf, axis=-1, keepdims=True) + NORM_EPS)
    return (y * w.astype(jnp.float32)).astype(x.dtype)


def causal_dwconv(u, w, b):
    L = u.shape[1]
    up = jnp.pad(u, ((0, 0), (CONV_WIDTH - 1, 0), (0, 0)))
    out = b
    for j in range(CONV_WIDTH):
        out = out + up[:, j:j + L, :] * w[j]
    return out


def ssd_chunked(xs, dt, A, Bg, Cg):
    b, l, h, p = xs.shape
    g, n = Bg.shape[2], Bg.shape[3]
    nc = l // CHUNK
    rep = h // g
    Bh = jnp.repeat(Bg, rep, axis=2).reshape(b, nc, CHUNK, h, n)
    Ch = jnp.repeat(Cg, rep, axis=2).reshape(b, nc, CHUNK, h, n)
    xdt = (xs * dt[..., None]).reshape(b, nc, CHUNK, h, p)
    a_cs = jnp.cumsum((dt * A).reshape(b, nc, CHUNK, h), axis=2)
    seg = a_cs[:, :, :, None, :] - a_cs[:, :, None, :, :]
    causal = jnp.tril(jnp.ones((CHUNK, CHUNK), dtype=bool))[None, None, :, :, None]
    decay_qs = jnp.exp(jnp.where(causal, seg, -jnp.inf))
    scores = jnp.einsum('bcqhn,bcshn->bcqsh', Ch, Bh) * decay_qs
    y_diag = jnp.einsum('bcqsh,bcshp->bcqhp', scores, xdt)
    decay_to_end = jnp.exp(a_cs[:, :, -1:, :] - a_cs)
    states = jnp.einsum('bcqhn,bcqhp->bchpn', Bh * decay_to_end[..., None], xdt)
    chunk_decay = jnp.exp(a_cs[:, :, -1, :])

    def step(carry, inp):
        st, dec = inp
        return carry * dec[:, :, None, None] + st, carry

    init = jnp.zeros((b, h, p, n), jnp.float32)
    _, prev = lax.scan(step, init, (jnp.moveaxis(states, 1, 0), jnp.moveaxis(chunk_decay, 1, 0)))
    prev = jnp.moveaxis(prev, 0, 1)
    y_off = jnp.einsum('bcqhn,bchpn->bcqhp', Ch * jnp.exp(a_cs)[..., None], prev)
    return (y_diag + y_off).reshape(b, l, h, p)


def mamba2_group(u, conv_w, conv_b, dt_bias, a_log, d_skip, norm_w):
    b, l, _ = u.shape
    xbc = jax.nn.silu(causal_dwconv(u[..., :CONV_DIM], conv_w, conv_b)).astype(jnp.float32)
    z = u[..., CONV_DIM:CONV_DIM + SSM_WIDTH].astype(jnp.float32)
    dt_raw = u[..., CONV_DIM + SSM_WIDTH:].astype(jnp.float32)
    xs = xbc[..., :SSM_WIDTH].reshape(b, l, SSM_HEADS, HEAD_DIM)
    Bg = xbc[..., SSM_WIDTH:SSM_WIDTH + SSM_GROUPS * SSM_STATE].reshape(b, l, SSM_GROUPS, SSM_STATE)
    Cg = xbc[..., SSM_WIDTH + SSM_GROUPS * SSM_STATE:].reshape(b, l, SSM_GROUPS, SSM_STATE)
    dt = jax.nn.softplus(dt_raw + dt_bias.astype(jnp.float32))
    A = -jnp.exp(a_log.astype(jnp.float32))
    y = ssd_chunked(xs, dt, A, Bg, Cg) + d_skip.astype(jnp.float32)[:, None] * xs
    y = y.reshape(b, l, SSM_WIDTH) * jax.nn.silu(z)
    yg = y.reshape(b, l, SSM_GROUPS, SSM_WIDTH // SSM_GROUPS)
    yg = yg * lax.rsqrt(jnp.mean(yg * yg, axis=-1, keepdims=True) + NORM_EPS)
    y = yg.reshape(b, l, SSM_WIDTH) * norm_w.astype(jnp.float32)
    return y.astype(u.dtype)


def rwkv7_recurrence(r, w, k, v, a_vec, b_vec):
    b, l, h, d = r.shape

    def step(S, inp):
        r_t, w_t, k_t, v_t, a_t, b_t = inp
        sa = jnp.einsum('bhij,bhj->bhi', S, a_t)
        S = S * w_t[:, :, None, :] + sa[..., None] * b_t[:, :, None, :] + v_t[..., None] * k_t[:, :, None, :]
        return S, jnp.einsum('bhij,bhj->bhi', S, r_t)

    S0 = jnp.zeros((b, h, d, d), jnp.float32)
    xs = (jnp.moveaxis(r, 1, 0), jnp.moveaxis(w, 1, 0), jnp.moveaxis(k, 1, 0),
          jnp.moveaxis(v, 1, 0), jnp.moveaxis(a_vec, 1, 0), jnp.moveaxis(b_vec, 1, 0))
    _, ys = lax.scan(step, S0, xs)
    return jnp.moveaxis(ys, 0, 1)


def rwkv7_group(u, mu, w0, w2, a0, a2, k_k, k_a, r_k, lnx_w, lnx_b):
    b, l, _ = u.shape
    prev = jnp.pad(u, ((0, 0), (1, 0), (0, 0)))[:, :l, :]
    us = (u + (prev - u) * mu).astype(jnp.float32)
    W = RWKV_WIDTH
    r, k, v, g = us[..., :W], us[..., W:2 * W], us[..., 2 * W:3 * W], us[..., 3 * W:4 * W]
    w_lat = us[..., 4 * W:4 * W + DECAY_LORA]
    a_lat = us[..., 4 * W + DECAY_LORA:]
    w_log = -jax.nn.softplus(-(w0.astype(jnp.float32) + jnp.tanh(w_lat) @ w2.astype(jnp.float32))) - 0.5
    decay = jnp.exp(-jnp.exp(w_log))
    a = jax.nn.sigmoid(a0.astype(jnp.float32) + a_lat @ a2.astype(jnp.float32))
    kk = (k * k_k.astype(jnp.float32)).reshape(b, l, RWKV_HEADS, HEAD_DIM)
    kk = kk / jnp.maximum(jnp.sqrt(jnp.sum(kk * kk, axis=-1, keepdims=True)), L2_EPS)
    k = k * (1.0 + (a - 1.0) * k_a.astype(jnp.float32))
    hs = lambda t: t.reshape(b, l, RWKV_HEADS, HEAD_DIM)
    rh, kh, vh, ah = hs(r), hs(k), hs(v), hs(a)
    y = rwkv7_recurrence(rh, hs(decay), kh, vh, -kk, kk * ah)
    mean = jnp.mean(y, axis=-1, keepdims=True)
    var = jnp.mean(jnp.square(y - mean), axis=-1, keepdims=True)
    y = (y - mean) * lax.rsqrt(var + LNX_EPS)
    y = y * lnx_w.astype(jnp.float32).reshape(RWKV_HEADS, HEAD_DIM) \
        + lnx_b.astype(jnp.float32).reshape(RWKV_HEADS, HEAD_DIM)
    bonus = jnp.sum(rh * kh * r_k.astype(jnp.float32), axis=-1, keepdims=True) * vh
    y = (y + bonus).reshape(b, l, W) * jax.nn.silu(g)
    return y.astype(u.dtype)


def memory_xattn_group(u, mem_k, mem_v):
    b, l, _ = u.shape
    q = u[..., :XATTN_WIDTH].reshape(b, l, XATTN_HEADS, HEAD_DIM)
    g = u[..., XATTN_WIDTH:]
    s = jnp.einsum('blhd,bmhd->bhlm', q.astype(jnp.float32), mem_k.astype(jnp.float32)) * (HEAD_DIM ** -0.5)
    p = jax.nn.softmax(s, axis=-1)
    o = jnp.einsum('bhlm,bmhd->blhd', p, mem_v.astype(jnp.float32)).reshape(b, l, XATTN_WIDTH)
    return (o * jax.nn.silu(g.astype(jnp.float32))).astype(u.dtype)


def setup_inputs(seed: int = 0) -> dict:
    key = jax.random.key(seed)
    ks = jax.random.split(key, 24)
    f32 = jnp.float32
    nrm = lambda k, shape, scale: jax.random.normal(k, shape, f32) * scale
    dt = jnp.exp(jax.random.uniform(ks[8], (DEPTH, SSM_HEADS), f32)
                 * (jnp.log(0.1) - jnp.log(0.001)) + jnp.log(0.001))
    return {
        "x": nrm(ks[0], (BATCH, SEQ, D_MODEL), 1.0),
        "mem": nrm(ks[1], (BATCH, MEM_LEN, D_MODEL), 1.0),
        "mem_norm_w": 1.0 + nrm(ks[2], (D_MODEL,), 0.02),
        "w_mem_kv": nrm(ks[3], (D_MODEL, 2 * XATTN_WIDTH), D_MODEL ** -0.5),
        "pre_norm_w": 1.0 + nrm(ks[4], (DEPTH, D_MODEL), 0.02),
        "w_in": nrm(ks[5], (DEPTH, D_MODEL, IN_WIDTH), D_MODEL ** -0.5),
        "conv_w": nrm(ks[6], (DEPTH, CONV_WIDTH, CONV_DIM), 0.5),
        "conv_b": nrm(ks[7], (DEPTH, CONV_DIM), 0.02),
        "dt_bias": dt + jnp.log(-jnp.expm1(-dt)),
        "a_log": jnp.log(jax.random.uniform(ks[9], (DEPTH, SSM_HEADS), f32, 1.0, 16.0)),
        "d_skip": 1.0 + nrm(ks[10], (DEPTH, SSM_HEADS), 0.1),
        "ssm_norm_w": 1.0 + nrm(ks[11], (DEPTH, SSM_WIDTH), 0.02),
        "shift_mu": jax.random.uniform(ks[12], (DEPTH, RWKV_IN), f32),
        "w0": jax.random.uniform(ks[13], (DEPTH, RWKV_WIDTH), f32, -4.0, 1.0),
        "w2": nrm(ks[14], (DEPTH, DECAY_LORA, RWKV_WIDTH), 0.1),
        "a0": nrm(ks[15], (DEPTH, RWKV_WIDTH), 0.1),
        "a2": nrm(ks[16], (DEPTH, AAA_LORA, RWKV_WIDTH), 0.3 * AAA_LORA ** -0.5),
        "k_k": 0.85 + nrm(ks[17], (DEPTH, RWKV_WIDTH), 0.02),
        "k_a": 1.0 + nrm(ks[18], (DEPTH, RWKV_WIDTH), 0.02),
        "r_k": nrm(ks[19], (DEPTH, RWKV_HEADS, HEAD_DIM), 0.1),
        "lnx_w": 1.0 + nrm(ks[20], (DEPTH, RWKV_WIDTH), 0.02),
        "lnx_b": nrm(ks[21], (DEPTH, RWKV_WIDTH), 0.02),
        "w_out": nrm(ks[22], (DEPTH, D_MIX, D_MODEL), D_MIX ** -0.5),
        "post_norm_w": 1.0 + nrm(ks[23], (DEPTH, D_MODEL), 0.02),
    }


def reference(x, mem, mem_norm_w, w_mem_kv, pre_norm_w, w_in, conv_w, conv_b, dt_bias, a_log,
              d_skip, ssm_norm_w, shift_mu, w0, w2, a0, a2, k_k, k_a, r_k, lnx_w, lnx_b,
              w_out, post_norm_w):
    b = mem.shape[0]
    kv = rmsnorm(mem, mem_norm_w) @ w_mem_kv
    mem_k = kv[..., :XATTN_WIDTH].reshape(b, MEM_LEN, XATTN_HEADS, HEAD_DIM)
    mem_v = kv[..., XATTN_WIDTH:].reshape(b, MEM_LEN, XATTN_HEADS, HEAD_DIM)
    for i in range(DEPTH):
        h = rmsnorm(x, pre_norm_w[i])
        u = h @ w_in[i]
        y_ssm = mamba2_group(u[..., :SSM_IN], conv_w[i], conv_b[i], dt_bias[i], a_log[i],
                             d_skip[i], ssm_norm_w[i])
        y_rwkv = rwkv7_group(u[..., SSM_IN:SSM_IN + RWKV_IN], shift_mu[i], w0[i], w2[i], a0[i],
                             a2[i], k_k[i], k_a[i], r_k[i], lnx_w[i], lnx_b[i])
        y_mem = memory_xattn_group(u[..., SSM_IN + RWKV_IN:], mem_k, mem_v)
        y = jnp.concatenate([y_ssm, y_rwkv, y_mem], axis=-1)
        x = x + rmsnorm(y @ w_out[i], post_norm_w[i])
    return x
```

```python
import functools

import jax
import jax.numpy as jnp
from jax import lax
from jax.experimental import pallas as pl
from jax.experimental.pallas import tpu as pltpu

F32 = jnp.float32
BF16 = jnp.bfloat16

D_MODEL = 1024
DEPTH = 4
HEAD_DIM = 64
CHUNK = 64
MEM_LEN = 256
SSM_WIDTH = 512
SSM_HEADS = 8
SSM_GROUPS = 2
SSM_STATE = 128
CONV_WIDTH = 4
CONV_DIM = SSM_WIDTH + 2 * SSM_GROUPS * SSM_STATE
SSM_IN = CONV_DIM + SSM_WIDTH + SSM_HEADS
RWKV_WIDTH = 256
RWKV_HEADS = 4
LORA = 64
RWKV_IN = 4 * RWKV_WIDTH + 2 * LORA
XATTN_WIDTH = 256
XATTN_HEADS = 4
XATTN_IN = 2 * XATTN_WIDTH
NORM_EPS = 1e-6
LNX_EPS = 64e-5
L2_EPS = 1e-12

N_SSM = CONV_DIM + SSM_WIDTH + SSM_WIDTH
N_IN = N_SSM + RWKV_IN + XATTN_IN

TB_PROJ = 512
TB_SCAN = 256
VMEM_LIMIT = 56 * 1024 * 1024


def _dot(a, b):
    return jnp.dot(a, b, preferred_element_type=F32)


def _dot_nt(a, b):
    return lax.dot_general(a, b, (((1,), (1,)), ((), ())), preferred_element_type=F32)


def _dot_tn(a, b):
    return lax.dot_general(a, b, (((0,), (0,)), ((), ())), preferred_element_type=F32)


def _split3(x):
    hi = x.astype(BF16)
    r1 = x - hi.astype(F32)
    mid = r1.astype(BF16)
    lo = (r1 - mid.astype(F32)).astype(BF16)
    return hi, mid, lo


def _dot_exact_lhs(lhs_bf16, x):
    hi, mid, lo = _split3(x)
    return _dot(lhs_bf16, lo) + _dot(lhs_bf16, mid) + _dot(lhs_bf16, hi)


def _sigmoid(x):
    return 1.0 / (1.0 + jnp.exp(-x))


def _silu(x):
    return x * _sigmoid(x)


def _softplus(x):
    return jnp.maximum(x, 0.0) + jnp.log1p(jnp.exp(-jnp.abs(x)))


def _iota(shape, dim):
    return lax.broadcasted_iota(jnp.int32, shape, dim)


def _block_tril(n):
    r = _iota((n, n), 0)
    c = _iota((n, n), 1)
    same = (r >> 6) == (c >> 6)
    return jnp.where(same & (c <= r), 1.0, 0.0).astype(BF16)


def _shift_rows(x, carry, j):
    rolled = pltpu.roll(x, j, 0)
    croll = pltpu.roll(carry, j, 0)
    row = _iota(carry.shape, 0)
    top = jnp.where(row < j, croll, rolled[:8])
    return jnp.concatenate([top, rolled[8:]], axis=0)


def _memkv_kernel(mem_ref, nw_ref, w_ref, o_ref):
    m = mem_ref[...]
    h = m * lax.rsqrt(jnp.mean(m * m, axis=-1, keepdims=True) + NORM_EPS) * nw_ref[...]
    o_ref[...] = _dot(h.astype(BF16), w_ref[...])


def _memkv(mem2d, nw, w_bf16):
    return pl.pallas_call(
        _memkv_kernel,
        out_shape=jax.ShapeDtypeStruct((MEM_LEN, 2 * XATTN_WIDTH), F32),
        name="memkv",
    )(mem2d, nw, w_bf16)


def _inproj_kernel(x_ref, nw_ref, w_ref, o_ssm, o_rw, o_xa):
    x = x_ref[...]
    h = x * lax.rsqrt(jnp.mean(x * x, axis=-1, keepdims=True) + NORM_EPS) * nw_ref[...]
    hb = h.astype(BF16)
    o_ssm[...] = _dot(hb, w_ref[:, :N_SSM])
    o_rw[...] = _dot(hb, w_ref[:, N_SSM:N_SSM + RWKV_IN])
    o_xa[...] = _dot(hb, w_ref[:, N_SSM + RWKV_IN:])


def _inproj(x2d, nw, w_bf16):
    t = x2d.shape[0]
    tb = min(TB_PROJ, t)
    full = lambda i: (0, 0)
    rows = lambda i: (i, 0)
    return pl.pallas_call(
        _inproj_kernel,
        grid=(t // tb,),
        in_specs=[pl.BlockSpec((tb, D_MODEL), rows),
                  pl.BlockSpec((1, D_MODEL), full),
                  pl.BlockSpec((D_MODEL, N_IN), full)],
        out_specs=[pl.BlockSpec((tb, N_SSM), rows),
                   pl.BlockSpec((tb, RWKV_IN), rows),
                   pl.BlockSpec((tb, XATTN_IN), rows)],
        out_shape=[jax.ShapeDtypeStruct((t, N_SSM), F32),
                   jax.ShapeDtypeStruct((t, RWKV_IN), F32),
                   jax.ShapeDtypeStruct((t, XATTN_IN), F32)],
        compiler_params=pltpu.CompilerParams(dimension_semantics=("parallel",),
                                             vmem_limit_bytes=VMEM_LIMIT),
        name="inproj",
    )(x2d, nw, w_bf16)


def _ssd_kernel(u_ref, cw_ref, cb_ref, dtb_ref, alog_ref, dsk_ref, nw_ref, o_ref,
                carry_ref, state_ref, y_ref):
    tb = u_ref.shape[0]

    @pl.when(pl.program_id(0) == 0)
    def _():
        carry_ref[...] = jnp.zeros_like(carry_ref)
        state_ref[...] = jnp.zeros_like(state_ref)

    pre = u_ref[:, :CONV_DIM]
    carry = carry_ref[...]
    conv = cb_ref[...] + pre * cw_ref[CONV_WIDTH - 1:CONV_WIDTH, :]
    for j in range(1, CONV_WIDTH):
        conv = conv + _shift_rows(pre, carry, j) * cw_ref[CONV_WIDTH - 1 - j:CONV_WIDTH - j, :]
    carry_ref[...] = pre[tb - 8:, :]
    xbc = _silu(conv)

    xs = xbc[:, :SSM_WIDTH]
    bmat = xbc[:, SSM_WIDTH:SSM_WIDTH + 256].astype(BF16)
    cmat = xbc[:, SSM_WIDTH + 256:].astype(BF16)
    dt = _softplus(u_ref[:, CONV_DIM + SSM_WIDTH:] + dtb_ref[...])
    da = dt * (-jnp.exp(alog_ref[...]))
    xdt = xs * dt
    a_cs = _dot_exact_lhs(_block_tril(tb), da)
    ea = jnp.exp(a_cs)

    q_i = _iota((CHUNK, 128), 0)
    s_i = _iota((CHUNK, 128), 1) & (CHUNK - 1)
    eye2 = q_i == s_i
    causal2 = s_i <= q_i
    bdmask = (_iota((128, 128), 0) >> 6) == (_iota((128, 128), 1) >> 6)

    for c in range(tb // CHUNK):
        rows = slice(c * CHUNK, (c + 1) * CHUNK)
        acs_c = a_cs[rows]
        a_last = acs_c[CHUNK - 1:CHUNK, :]
        xdt_c = xdt[rows]
        xend_c = (xdt_c * jnp.exp(a_last - acs_c)).astype(BF16)
        chunk_decay = jnp.exp(a_last)
        y_parts = []
        for g in range(SSM_GROUPS):
            cg = cmat[rows, g * 128:(g + 1) * 128]
            bg = bmat[rows, g * 128:(g + 1) * 128]
            gp = _dot_nt(cg, jnp.concatenate([bg, bg], axis=0))
            st_g = state_ref[:, g * 256:(g + 1) * 256]
            y_off = _dot(cg, st_g.astype(BF16)) * ea[rows, g * 256:(g + 1) * 256]
            y_diag = []
            for jj in range(2):
                lanes = slice(g * 256 + jj * 128, g * 256 + (jj + 1) * 128)
                colp = acs_c[:, lanes]
                rowp = jnp.sum(jnp.where(eye2, colp, 0.0), axis=0, keepdims=True)
                dm = jnp.exp(jnp.where(causal2, colp - rowp, -1e30))
                m = (gp * dm).astype(BF16)
                xd = xdt_c[:, lanes]
                bd = jnp.where(bdmask, jnp.concatenate([xd, xd], axis=0), 0.0).astype(BF16)
                y_diag.append(_dot(m, bd))
            y_parts.append(jnp.concatenate(y_diag, axis=1) + y_off)
            state_ref[:, g * 256:(g + 1) * 256] = (
                st_g * chunk_decay[:, g * 256:(g + 1) * 256]
                + _dot_tn(bg, xend_c[:, g * 256:(g + 1) * 256]))
        y_ref[rows, :] = jnp.concatenate(y_parts, axis=1)

    y = y_ref[...] + dsk_ref[...] * xs
    y = y * _silu(u_ref[:, CONV_DIM:CONV_DIM + SSM_WIDTH])
    outs = []
    for g in range(SSM_GROUPS):
        yg = y[:, g * 256:(g + 1) * 256]
        outs.append(yg * lax.rsqrt(jnp.mean(yg * yg, axis=-1, keepdims=True) + NORM_EPS))
    o_ref[...] = (jnp.concatenate(outs, axis=1) * nw_ref[...]).astype(o_ref.dtype)


def _ssd(u_ssm, cw, cb, dtb, alog, dsk, nw):
    t = u_ssm.shape[0]
    tb = min(TB_SCAN, t)
    full = lambda i: (0, 0)
    rows = lambda i: (i, 0)
    return pl.pallas_call(
        _ssd_kernel,
        grid=(t // tb,),
        in_specs=[pl.BlockSpec((tb, N_SSM), rows),
                  pl.BlockSpec((CONV_WIDTH, CONV_DIM), full),
                  pl.BlockSpec((1, CONV_DIM), full),
                  pl.BlockSpec((1, SSM_WIDTH), full),
                  pl.BlockSpec((1, SSM_WIDTH), full),
                  pl.BlockSpec((1, SSM_WIDTH), full),
                  pl.BlockSpec((1, SSM_WIDTH), full)],
        out_specs=pl.BlockSpec((tb, SSM_WIDTH), rows),
        out_shape=jax.ShapeDtypeStruct((t, SSM_WIDTH), BF16),
        scratch_shapes=[pltpu.VMEM((8, CONV_DIM), F32),
                        pltpu.VMEM((SSM_STATE, SSM_WIDTH), F32),
                        pltpu.VMEM((tb, SSM_WIDTH), F32)],
        compiler_params=pltpu.CompilerParams(dimension_semantics=("arbitrary",),
                                             vmem_limit_bytes=VMEM_LIMIT),
        name="ssd",
    )(u_ssm, cw, cb, dtb, alog, dsk, nw)


def _rwkv_kernel(u_ref, mu_ref, w2a_ref, w0_ref, a0_ref, kk_ref, ka_ref, rk_ref, lw_ref, lb_ref,
                 o_ref, carry_ref, state_ref, y_ref):
    tb = u_ref.shape[0]
    W = RWKV_WIDTH

    @pl.when(pl.program_id(0) == 0)
    def _():
        carry_ref[...] = jnp.zeros_like(carry_ref)
        state_ref[...] = jnp.zeros_like(state_ref)

    u = u_ref[...]
    prev = _shift_rows(u, carry_ref[...], 1)
    carry_ref[...] = u[tb - 8:, :]
    us = u + (prev - u) * mu_ref[...]
    r = us[:, :W]
    k = us[:, W:2 * W]
    v = us[:, 2 * W:3 * W]
    g = us[:, 3 * W:4 * W]
    lat = us[:, 4 * W:]
    lat = jnp.where(_iota(lat.shape, 1) < LORA, jnp.tanh(lat), lat)
    lora = _dot(lat.astype(BF16), w2a_ref[...])
    w_log = -_softplus(-(w0_ref[...] + lora[:, :W])) - 0.5
    logw = -jnp.exp(w_log)
    a = _sigmoid(a0_ref[...] + lora[:, W:])

    lane = _iota((W, W), 1) >> 6
    blk = ((_iota((W, W), 0) >> 6) == lane)
    head_ones = jnp.where(blk, 1.0, 0.0).astype(BF16)
    head_avg = jnp.where(blk, 1.0 / HEAD_DIM, 0.0).astype(BF16)

    kkr = k * kk_ref[...]
    ss = _dot((kkr * kkr).astype(BF16), head_ones)
    kk = kkr / jnp.maximum(jnp.sqrt(ss), L2_EPS)
    k2 = k * (1.0 + (a - 1.0) * ka_ref[...])
    b_vec = kk * a

    cum = _dot_exact_lhs(_block_tril(tb), logw)
    e_c = jnp.exp(cum)
    e_n = jnp.exp(-cum)
    at = -kk * jnp.exp(cum - logw)
    rt = r * e_c
    bt = b_vec * e_n
    kt = k2 * e_n

    lane128 = _iota((CHUNK, 128), 1)
    left = lane128 < CHUNK
    t_i = _iota((CHUNK, 128), 0)
    j_i = lane128 & (CHUNK - 1)
    strict = j_i < t_i
    incl = j_i <= t_i
    eye_l = left & (j_i == t_i)
    head_of_lane = _iota((CHUNK, W), 1) >> 6
    head_of_lane2 = _iota((2 * CHUNK, W), 1) >> 6

    for c in range(tb // CHUNK):
        rows = slice(c * CHUNK, (c + 1) * CHUNK)
        cum_c = cum[rows]
        cum_last = cum_c[CHUNK - 1:CHUNK, :]
        e_end = jnp.exp(cum_last - cum_c)
        v_c = v[rows]
        s_bd = state_ref[...]
        x_cat = jnp.concatenate([at[rows], rt[rows]], axis=0)
        y_cat = jnp.concatenate([kt[rows], bt[rows]], axis=0).astype(BF16)
        sa = _dot_nt(x_cat.astype(BF16), s_bd.astype(BF16))

        lhs_ak, lhs_t, lhs_r = [], [], []
        for h in range(RWKV_HEADS):
            xm = jnp.where(head_of_lane2 == h, x_cat, 0.0).astype(BF16)
            aa = _dot_nt(xm, y_cat)
            top = jnp.where(strict, aa[:CHUNK], 0.0)
            lhs_r.append(jnp.where(incl, aa[CHUNK:], 0.0))
            lhs_ak.append(jnp.where(left, top, 0.0))
            tp = jnp.where(left, jnp.where(eye_l, 1.0, 0.0), top)
            for _ in range(6):
                lhs = jnp.where(left, 0.0, tp).astype(BF16)
                rhs = jnp.concatenate([tp, tp], axis=0).astype(BF16)
                tp = _dot(lhs, rhs) + jnp.where(left, tp, 0.0)
            lhs_t.append(jnp.where(left, tp, 0.0))

        def stack(first, second):
            parts = []
            for h in range(RWKV_HEADS):
                m = head_of_lane == h
                parts.append(jnp.where(m, first, 0.0))
                parts.append(jnp.where(m, second, 0.0))
            return jnp.concatenate(parts, axis=0).astype(BF16)

        rhs0 = sa[:CHUNK] + _dot(jnp.concatenate(lhs_ak, axis=1).astype(BF16), stack(v_c, v_c))
        u_c = _dot(jnp.concatenate(lhs_t, axis=1).astype(BF16), stack(rhs0, rhs0))
        y_ref[rows, :] = sa[CHUNK:] + _dot(jnp.concatenate(lhs_r, axis=1).astype(BF16),
                                           stack(v_c, u_c))
        uv = jnp.concatenate([u_c, v_c], axis=0).astype(BF16)
        bk = jnp.concatenate([b_vec[rows] * e_end, k2[rows] * e_end], axis=0).astype(BF16)
        state_ref[...] = s_bd * jnp.exp(cum_last) + jnp.where(blk, _dot_tn(uv, bk), 0.0)

    y = y_ref[...]
    mean = _dot(y.astype(BF16), head_avg)
    yc = y - mean
    var = _dot((yc * yc).astype(BF16), head_avg)
    yn = yc * lax.rsqrt(var + LNX_EPS) * lw_ref[...] + lb_ref[...]
    bonus = _dot((r * k2 * rk_ref[...]).astype(BF16), head_ones) * v
    o_ref[...] = ((yn + bonus) * _silu(g)).astype(o_ref.dtype)


def _rwkv(u_rw, mu, w2a, w0, a0, k_k, k_a, r_k, lnx_w, lnx_b):
    t = u_rw.shape[0]
    tb = min(TB_SCAN, t)
    full = lambda i: (0, 0)
    rows = lambda i: (i, 0)
    vec = pl.BlockSpec((1, RWKV_WIDTH), full)
    return pl.pallas_call(
        _rwkv_kernel,
        grid=(t // tb,),
        in_specs=[pl.BlockSpec((tb, RWKV_IN), rows),
                  pl.BlockSpec((1, RWKV_IN), full),
                  pl.BlockSpec((2 * LORA, 2 * RWKV_WIDTH), full),
                  vec, vec, vec, vec, vec, vec, vec],
        out_specs=pl.BlockSpec((tb, RWKV_WIDTH), rows),
        out_shape=jax.ShapeDtypeStruct((t, RWKV_WIDTH), BF16),
        scratch_shapes=[pltpu.VMEM((8, RWKV_IN), F32),
                        pltpu.VMEM((RWKV_WIDTH, RWKV_WIDTH), F32),
                        pltpu.VMEM((tb, RWKV_WIDTH), F32)],
        compiler_params=pltpu.CompilerParams(dimension_semantics=("arbitrary",),
                                             vmem_limit_bytes=VMEM_LIMIT),
        name="rwkv",
    )(u_rw, mu, w2a, w0, a0, k_k, k_a, r_k, lnx_w, lnx_b)


def _outproj_kernel(x_ref, ys_ref, yr_ref, ux_ref, kt_ref, vb_ref, ones_ref, w_ref, nw_ref, o_ref):
    q = ux_ref[:, :XATTN_WIDTH].astype(BF16)
    s = _dot(q, kt_ref[...])
    ps = []
    for h in range(XATTN_HEADS):
        sh = s[:, h * MEM_LEN:(h + 1) * MEM_LEN]
        ps.append(jnp.exp(sh - jnp.max(sh, axis=-1, keepdims=True)))
    p = jnp.concatenate(ps, axis=1).astype(BF16)
    o = _dot(p, vb_ref[...]) / _dot(p, ones_ref[...])
    o = o * _silu(ux_ref[:, XATTN_WIDTH:])
    y = jnp.concatenate([ys_ref[...], yr_ref[...], o.astype(BF16)], axis=1)
    d = _dot(y, w_ref[...])
    d = d * lax.rsqrt(jnp.mean(d * d, axis=-1, keepdims=True) + NORM_EPS) * nw_ref[...]
    o_ref[...] = x_ref[...] + d


def _outproj(x2d, y_ssm, y_rw, u_xa, kt_bd, v_bd, ones_bd, w_bf16, nw):
    t = x2d.shape[0]
    tb = min(TB_PROJ, t)
    full = lambda i: (0, 0)
    rows = lambda i: (i, 0)
    return pl.pallas_call(
        _outproj_kernel,
        grid=(t // tb,),
        in_specs=[pl.BlockSpec((tb, D_MODEL), rows),
                  pl.BlockSpec((tb, SSM_WIDTH), rows),
                  pl.BlockSpec((tb, RWKV_WIDTH), rows),
                  pl.BlockSpec((tb, XATTN_IN), rows),
                  pl.BlockSpec((XATTN_WIDTH, XATTN_HEADS * MEM_LEN), full),
                  pl.BlockSpec((XATTN_HEADS * MEM_LEN, XATTN_WIDTH), full),
                  pl.BlockSpec((XATTN_HEADS * MEM_LEN, XATTN_WIDTH), full),
                  pl.BlockSpec((D_MODEL, D_MODEL), full),
                  pl.BlockSpec((1, D_MODEL), full)],
        out_specs=pl.BlockSpec((tb, D_MODEL), rows),
        out_shape=jax.ShapeDtypeStruct((t, D_MODEL), F32),
        compiler_params=pltpu.CompilerParams(dimension_semantics=("parallel",),
                                             vmem_limit_bytes=VMEM_LIMIT),
        name="outproj",
    )(x2d, y_ssm, y_rw, u_xa, kt_bd, v_bd, ones_bd, w_bf16, nw)


def _block_diag_heads(blocks):
    h, r, c = blocks.shape
    eye = jnp.eye(h, dtype=blocks.dtype)
    return (eye[:, None, :, None] * blocks[:, :, None, :]).reshape(h * r, h * c)


def kernel(x, mem, mem_norm_w, w_mem_kv, pre_norm_w, w_in, conv_w, conv_b, dt_bias, a_log, d_skip,
           ssm_norm_w, shift_mu, w0, w2, a0, a2, k_k, k_a, r_k, lnx_w, lnx_b, w_out, post_norm_w):
    b, t, _ = x.shape
    assert b == 1 and mem.shape[0] == 1
    rep = lambda p: jnp.repeat(p, HEAD_DIM, axis=-1)

    kv = _memkv(mem[0], mem_norm_w[None, :], w_mem_kv.astype(BF16))
    mk = kv[:, :XATTN_WIDTH].reshape(MEM_LEN, XATTN_HEADS, HEAD_DIM)
    mv = kv[:, XATTN_WIDTH:].reshape(MEM_LEN, XATTN_HEADS, HEAD_DIM)
    kt_bd = _block_diag_heads(jnp.transpose(mk, (1, 2, 0)) * (HEAD_DIM ** -0.5)).astype(BF16)
    v_bd = _block_diag_heads(jnp.transpose(mv, (1, 0, 2))).astype(BF16)
    ones_bd = _block_diag_heads(jnp.ones((XATTN_HEADS, MEM_LEN, HEAD_DIM), BF16))

    w_dt = jnp.repeat(w_in[:, :, CONV_DIM + SSM_WIDTH:SSM_IN], HEAD_DIM, axis=-1)
    w_in_k = jnp.concatenate([w_in[:, :, :CONV_DIM + SSM_WIDTH], w_dt, w_in[:, :, SSM_IN:]],
                             axis=-1).astype(BF16)
    w_out_k = w_out.astype(BF16)
    zeros = jnp.zeros((DEPTH, LORA, RWKV_WIDTH), F32)
    w2a = jnp.concatenate([jnp.concatenate([w2, zeros], axis=-1),
                           jnp.concatenate([zeros, a2], axis=-1)], axis=1).astype(BF16)

    xc = x[0]
    for i in range(DEPTH):
        u_ssm, u_rw, u_xa = _inproj(xc, pre_norm_w[i][None, :], w_in_k[i])
        y_ssm = _ssd(u_ssm, conv_w[i], conv_b[i][None, :], rep(dt_bias[i])[None, :],
                     rep(a_log[i])[None, :], rep(d_skip[i])[None, :], ssm_norm_w[i][None, :])
        y_rw = _rwkv(u_rw, shift_mu[i][None, :], w2a[i], w0[i][None, :], a0[i][None, :],
                     k_k[i][None, :], k_a[i][None, :], r_k[i].reshape(1, RWKV_WIDTH),
                     lnx_w[i][None, :], lnx_b[i][None, :])
        xc = _outproj(xc, y_ssm, y_rw, u_xa, kt_bd, v_bd, ones_bd, w_out_k[i],
                      post_norm_w[i][None, :])
    return xc[None]
```

```python
import functools

import jax
import jax.numpy as jnp
from jax import lax
from jax.experimental import pallas as pl
from jax.experimental.pallas import tpu as pltpu

F32 = jnp.float32
BF16 = jnp.bfloat16

D_MODEL = 1024
DEPTH = 4
HEAD_DIM = 64
CHUNK = 64
MEM_LEN = 256
SSM_WIDTH = 512
SSM_HEADS = 8
SSM_GROUPS = 2
SSM_STATE = 128
CONV_WIDTH = 4
CONV_DIM = SSM_WIDTH + 2 * SSM_GROUPS * SSM_STATE
SSM_IN = CONV_DIM + SSM_WIDTH + SSM_HEADS
RWKV_WIDTH = 256
RWKV_HEADS = 4
LORA = 64
RWKV_IN = 4 * RWKV_WIDTH + 2 * LORA
XATTN_WIDTH = 256
XATTN_HEADS = 4
XATTN_IN = 2 * XATTN_WIDTH
NORM_EPS = 1e-6
LNX_EPS = 64e-5
L2_EPS = 1e-12

N_SSM = CONV_DIM + SSM_WIDTH + SSM_WIDTH
N_IN = N_SSM + RWKV_IN + XATTN_IN

TB_PROJ = 512
TB_SCAN = 256
TB_RWKV = 512
VMEM_LIMIT = 56 * 1024 * 1024


def _dot(a, b):
    return jnp.dot(a, b, preferred_element_type=F32)


def _dot_nt(a, b):
    return lax.dot_general(a, b, (((1,), (1,)), ((), ())), preferred_element_type=F32)


def _dot_tn(a, b):
    return lax.dot_general(a, b, (((0,), (0,)), ((), ())), preferred_element_type=F32)


def _split3(x):
    hi = x.astype(BF16)
    r1 = x - hi.astype(F32)
    mid = r1.astype(BF16)
    lo = (r1 - mid.astype(F32)).astype(BF16)
    return hi, mid, lo


def _dot_exact_lhs(lhs_bf16, x):
    hi, mid, lo = _split3(x)
    return _dot(lhs_bf16, lo) + _dot(lhs_bf16, mid) + _dot(lhs_bf16, hi)


def _sigmoid(x):
    return 1.0 / (1.0 + jnp.exp(-x))


def _silu(x):
    return x * _sigmoid(x)


def _softplus(x):
    return jnp.maximum(x, 0.0) + jnp.log1p(jnp.exp(-jnp.abs(x)))


def _iota(shape, dim):
    return lax.broadcasted_iota(jnp.int32, shape, dim)


def _block_tril(n):
    r = _iota((n, n), 0)
    c = _iota((n, n), 1)
    same = (r >> 6) == (c >> 6)
    return jnp.where(same & (c <= r), 1.0, 0.0).astype(BF16)


def _shift_rows(x, carry, j):
    rolled = pltpu.roll(x, j, 0)
    croll = pltpu.roll(carry, j, 0)
    row = _iota(carry.shape, 0)
    top = jnp.where(row < j, croll, rolled[:8])
    return jnp.concatenate([top, rolled[8:]], axis=0)


def _memkv_kernel(mem_ref, nw_ref, w_ref, o_ref):
    m = mem_ref[...]
    h = m * lax.rsqrt(jnp.mean(m * m, axis=-1, keepdims=True) + NORM_EPS) * nw_ref[...]
    o_ref[...] = _dot(h.astype(BF16), w_ref[...])


def _memkv(mem2d, nw, w_bf16):
    return pl.pallas_call(
        _memkv_kernel,
        out_shape=jax.ShapeDtypeStruct((MEM_LEN, 2 * XATTN_WIDTH), F32),
        name="memkv",
    )(mem2d, nw, w_bf16)


def _inproj_kernel(x_ref, nw_ref, w_ref, o_ssm, o_rw, o_xa):
    x = x_ref[...]
    h = x * lax.rsqrt(jnp.mean(x * x, axis=-1, keepdims=True) + NORM_EPS) * nw_ref[...]
    hb = h.astype(BF16)
    o_ssm[...] = _dot(hb, w_ref[:, :N_SSM])
    o_rw[...] = _dot(hb, w_ref[:, N_SSM:N_SSM + RWKV_IN])
    o_xa[...] = _dot(hb, w_ref[:, N_SSM + RWKV_IN:])


def _inproj(x2d, nw, w_bf16):
    t = x2d.shape[0]
    tb = min(TB_PROJ, t)
    full = lambda i: (0, 0)
    rows = lambda i: (i, 0)
    return pl.pallas_call(
        _inproj_kernel,
        grid=(t // tb,),
        in_specs=[pl.BlockSpec((tb, D_MODEL), rows),
                  pl.BlockSpec((1, D_MODEL), full),
                  pl.BlockSpec((D_MODEL, N_IN), full)],
        out_specs=[pl.BlockSpec((tb, N_SSM), rows),
                   pl.BlockSpec((tb, RWKV_IN), rows),
                   pl.BlockSpec((tb, XATTN_IN), rows)],
        out_shape=[jax.ShapeDtypeStruct((t, N_SSM), F32),
                   jax.ShapeDtypeStruct((t, RWKV_IN), F32),
                   jax.ShapeDtypeStruct((t, XATTN_IN), F32)],
        compiler_params=pltpu.CompilerParams(dimension_semantics=("parallel",),
                                             vmem_limit_bytes=VMEM_LIMIT),
        name="inproj",
    )(x2d, nw, w_bf16)


def _ssd_kernel(u_ref, cw_ref, cb_ref, dtb_ref, alog_ref, dsk_ref, nw_ref, o_ref,
                carry_ref, state_ref, y_ref):
    tb = u_ref.shape[0]

    @pl.when(pl.program_id(0) == 0)
    def _():
        carry_ref[...] = jnp.zeros_like(carry_ref)
        state_ref[...] = jnp.zeros_like(state_ref)

    pre = u_ref[:, :CONV_DIM]
    carry = carry_ref[...]
    conv = cb_ref[...] + pre * cw_ref[CONV_WIDTH - 1:CONV_WIDTH, :]
    for j in range(1, CONV_WIDTH):
        conv = conv + _shift_rows(pre, carry, j) * cw_ref[CONV_WIDTH - 1 - j:CONV_WIDTH - j, :]
    carry_ref[...] = pre[tb - 8:, :]
    xbc = _silu(conv)

    xs = xbc[:, :SSM_WIDTH]
    bmat = xbc[:, SSM_WIDTH:SSM_WIDTH + 256].astype(BF16)
    cmat = xbc[:, SSM_WIDTH + 256:].astype(BF16)
    dt = _softplus(u_ref[:, CONV_DIM + SSM_WIDTH:] + dtb_ref[...])
    da = dt * (-jnp.exp(alog_ref[...]))
    xdt = xs * dt
    a_cs = _dot_exact_lhs(_block_tril(tb), da)
    ea = jnp.exp(a_cs)

    q_i = _iota((CHUNK, 128), 0)
    s_i = _iota((CHUNK, 128), 1) & (CHUNK - 1)
    eye2 = q_i == s_i
    causal2 = s_i <= q_i
    bdmask = (_iota((128, 128), 0) >> 6) == (_iota((128, 128), 1) >> 6)

    for c in range(tb // CHUNK):
        rows = slice(c * CHUNK, (c + 1) * CHUNK)
        acs_c = a_cs[rows]
        a_last = acs_c[CHUNK - 1:CHUNK, :]
        xdt_c = xdt[rows]
        xend_c = (xdt_c * jnp.exp(a_last - acs_c)).astype(BF16)
        chunk_decay = jnp.exp(a_last)
        y_parts = []
        for g in range(SSM_GROUPS):
            cg = cmat[rows, g * 128:(g + 1) * 128]
            bg = bmat[rows, g * 128:(g + 1) * 128]
            gp = _dot_nt(cg, jnp.concatenate([bg, bg], axis=0))
            st_g = state_ref[:, g * 256:(g + 1) * 256]
            y_off = _dot(cg, st_g.astype(BF16)) * ea[rows, g * 256:(g + 1) * 256]
            y_diag = []
            for jj in range(2):
                lanes = slice(g * 256 + jj * 128, g * 256 + (jj + 1) * 128)
                colp = acs_c[:, lanes]
                rowp = jnp.sum(jnp.where(eye2, colp, 0.0), axis=0, keepdims=True)
                dm = jnp.exp(jnp.where(causal2, colp - rowp, -1e30))
                m = (gp * dm).astype(BF16)
                xd = xdt_c[:, lanes]
                bd = jnp.where(bdmask, jnp.concatenate([xd, xd], axis=0), 0.0).astype(BF16)
                y_diag.append(_dot(m, bd))
            y_parts.append(jnp.concatenate(y_diag, axis=1) + y_off)
            state_ref[:, g * 256:(g + 1) * 256] = (
                st_g * chunk_decay[:, g * 256:(g + 1) * 256]
                + _dot_tn(bg, xend_c[:, g * 256:(g + 1) * 256]))
        y_ref[rows, :] = jnp.concatenate(y_parts, axis=1)

    y = y_ref[...] + dsk_ref[...] * xs
    y = y * _silu(u_ref[:, CONV_DIM:CONV_DIM + SSM_WIDTH])
    outs = []
    for g in range(SSM_GROUPS):
        yg = y[:, g * 256:(g + 1) * 256]
        outs.append(yg * lax.rsqrt(jnp.mean(yg * yg, axis=-1, keepdims=True) + NORM_EPS))
    o_ref[...] = (jnp.concatenate(outs, axis=1) * nw_ref[...]).astype(o_ref.dtype)


def _ssd(u_ssm, cw, cb, dtb, alog, dsk, nw):
    t = u_ssm.shape[0]
    tb = min(TB_SCAN, t)
    full = lambda i: (0, 0)
    rows = lambda i: (i, 0)
    return pl.pallas_call(
        _ssd_kernel,
        grid=(t // tb,),
        in_specs=[pl.BlockSpec((tb, N_SSM), rows),
                  pl.BlockSpec((CONV_WIDTH, CONV_DIM), full),
                  pl.BlockSpec((1, CONV_DIM), full),
                  pl.BlockSpec((1, SSM_WIDTH), full),
                  pl.BlockSpec((1, SSM_WIDTH), full),
                  pl.BlockSpec((1, SSM_WIDTH), full),
                  pl.BlockSpec((1, SSM_WIDTH), full)],
        out_specs=pl.BlockSpec((tb, SSM_WIDTH), rows),
        out_shape=jax.ShapeDtypeStruct((t, SSM_WIDTH), BF16),
        scratch_shapes=[pltpu.VMEM((8, CONV_DIM), F32),
                        pltpu.VMEM((SSM_STATE, SSM_WIDTH), F32),
                        pltpu.VMEM((tb, SSM_WIDTH), F32)],
        compiler_params=pltpu.CompilerParams(dimension_semantics=("arbitrary",),
                                             vmem_limit_bytes=VMEM_LIMIT),
        name="ssd",
    )(u_ssm, cw, cb, dtb, alog, dsk, nw)


def _rwkv_kernel(u_ref, mu_ref, w2a_ref, w0_ref, a0_ref, kk_ref, ka_ref, rk_ref, lw_ref, lb_ref,
                 o_ref, carry_ref, state_ref, y_ref):
    tb = u_ref.shape[0]
    W = RWKV_WIDTH

    @pl.when(pl.program_id(0) == 0)
    def _():
        carry_ref[...] = jnp.zeros_like(carry_ref)
        state_ref[...] = jnp.zeros_like(state_ref)

    u = u_ref[...]
    prev = _shift_rows(u, carry_ref[...], 1)
    carry_ref[...] = u[tb - 8:, :]
    us = u + (prev - u) * mu_ref[...]
    r = us[:, :W]
    k = us[:, W:2 * W]
    v = us[:, 2 * W:3 * W]
    g = us[:, 3 * W:4 * W]
    lat = us[:, 4 * W:]
    lat = jnp.where(_iota(lat.shape, 1) < LORA, jnp.tanh(lat), lat)
    lora = _dot(lat.astype(BF16), w2a_ref[...])
    w_log = -_softplus(-(w0_ref[...] + lora[:, :W])) - 0.5
    logw = -jnp.exp(w_log)
    a = _sigmoid(a0_ref[...] + lora[:, W:])

    lane = _iota((W, W), 1) >> 6
    blk = ((_iota((W, W), 0) >> 6) == lane)
    head_ones = jnp.where(blk, 1.0, 0.0).astype(BF16)
    head_avg = jnp.where(blk, 1.0 / HEAD_DIM, 0.0).astype(BF16)

    kkr = k * kk_ref[...]
    ss = _dot((kkr * kkr).astype(BF16), head_ones)
    kk = kkr / jnp.maximum(jnp.sqrt(ss), L2_EPS)
    k2 = k * (1.0 + (a - 1.0) * ka_ref[...])
    b_vec = kk * a

    cum = _dot_exact_lhs(_block_tril(tb), logw)
    e_c = jnp.exp(cum)
    e_n = jnp.exp(-cum)
    at = -kk * jnp.exp(cum - logw)
    rt = r * e_c
    bt = b_vec * e_n
    kt = k2 * e_n

    nc = tb // CHUNK
    lane512 = _iota((CHUNK, 4 * 128), 1)
    left = (lane512 & 127) < CHUNK
    t_i = _iota((CHUNK, 4 * 128), 0)
    j_i = lane512 & (CHUNK - 1)
    strict_left = (j_i < t_i) & left
    strict = j_i < t_i
    incl = j_i <= t_i
    eye_left = jnp.where(left & (j_i == t_i), 1.0, 0.0)
    head_of_lane = _iota((CHUNK, W), 1) >> 6
    head_masks = [head_of_lane == h for h in range(RWKV_HEADS)]
    head_masks2 = [(_iota((2 * CHUNK, W), 1) >> 6) == h for h in range(RWKV_HEADS)]
    r128 = _iota((CHUNK, 128), 0)
    c128 = _iota((CHUNK, 128), 1)
    eye0 = jnp.where(r128 == c128, 1.0, 0.0).astype(BF16)
    zero_blk = jnp.zeros((CHUNK, W), BF16)

    def hmask(x, h):
        return jnp.where(head_masks[h], x, jnp.zeros_like(x))

    tps, lhs_ak, lhs_r = [], [], []
    for c in range(nc):
        rows = slice(c * CHUNK, (c + 1) * CHUNK)
        x_cat = jnp.concatenate([at[rows], rt[rows]], axis=0).astype(BF16)
        y_cat = jnp.concatenate([kt[rows], bt[rows]], axis=0).astype(BF16)
        y_stack = jnp.concatenate(
            [jnp.where(head_masks2[h], y_cat, jnp.zeros_like(y_cat)) for h in range(RWKV_HEADS)], axis=0)
        aa = _dot_nt(x_cat, y_stack)
        top = aa[:CHUNK]
        lhs_ak.append(jnp.where(strict_left, top, 0.0).astype(BF16))
        lhs_r.append(jnp.where(incl, aa[CHUNK:], 0.0).astype(BF16))
        tp0 = jnp.where(left, eye_left, jnp.where(strict, top, 0.0)).astype(BF16)
        tps.append([tp0[:, h * 128:(h + 1) * 128] for h in range(RWKV_HEADS)])

    for step in range(6):
        for c in range(nc):
            for h in range(RWKV_HEADS):
                tp = tps[c][h]
                tps[c][h] = _dot(tp, jnp.concatenate([eye0, tp], axis=0)).astype(BF16)

    qs, y0s, ps, ns, decays = [], [], [], [], []
    for c in range(nc):
        rows = slice(c * CHUNK, (c + 1) * CHUNK)
        cum_c = cum[rows]
        cum_last = cum_c[CHUNK - 1:CHUNK, :]
        e_end = jnp.exp(cum_last - cum_c)
        decays.append(jnp.exp(cum_last))
        v_c = v[rows].astype(BF16)
        at_c = at[rows].astype(BF16)
        vm = [hmask(v_c, h) for h in range(RWKV_HEADS)]
        av = _dot(lhs_ak[c], jnp.concatenate([vm[h // 2] for h in range(2 * RWKV_HEADS)], axis=0))
        av = av.astype(BF16)
        wu_rhs = []
        for h in range(RWKV_HEADS):
            blk_h = jnp.concatenate([hmask(at_c, h), hmask(av, h)], axis=1)
            wu_rhs += [blk_h, blk_h]
        lhs_t = jnp.concatenate(tps[c], axis=1)
        wu = _dot(lhs_t, jnp.concatenate(wu_rhs, axis=0)).astype(BF16)
        w_c = wu[:, :W]
        u0_c = wu[:, W:]
        qy_rhs = []
        for h in range(RWKV_HEADS):
            qy_rhs.append(jnp.concatenate([zero_blk, vm[h]], axis=1))
            qy_rhs.append(jnp.concatenate([hmask(w_c, h), hmask(u0_c, h)], axis=1))
        qy = _dot(lhs_r[c], jnp.concatenate(qy_rhs, axis=0))
        qs.append((rt[rows] + qy[:, :W]).astype(BF16))
        y0s.append(qy[:, W:])
        mn_lhs = jnp.concatenate([wu, jnp.concatenate([zero_blk, v_c], axis=1)], axis=0)
        bk = jnp.concatenate([b_vec[rows] * e_end, k2[rows] * e_end], axis=0).astype(BF16)
        mn = _dot_tn(mn_lhs, bk)
        ps.append(jnp.where(blk, mn[:W], 0.0).astype(BF16))
        ns.append(jnp.where(blk, mn[W:], 0.0))

    s_bd = state_ref[...]
    for c in range(nc):
        rows = slice(c * CHUNK, (c + 1) * CHUNK)
        sb = s_bd.astype(BF16)
        y_ref[rows, :] = _dot_nt(qs[c], sb) + y0s[c]
        s_bd = s_bd * decays[c] + _dot(sb, ps[c]) + ns[c]
    state_ref[...] = s_bd

    y = y_ref[...]
    mean = _dot(y.astype(BF16), head_avg)
    yc = y - mean
    var = _dot((yc * yc).astype(BF16), head_avg)
    yn = yc * lax.rsqrt(var + LNX_EPS) * lw_ref[...] + lb_ref[...]
    bonus = _dot((r * k2 * rk_ref[...]).astype(BF16), head_ones) * v
    o_ref[...] = ((yn + bonus) * _silu(g)).astype(o_ref.dtype)


def _rwkv(u_rw, mu, w2a, w0, a0, k_k, k_a, r_k, lnx_w, lnx_b):
    t = u_rw.shape[0]
    tb = min(TB_RWKV, t)
    full = lambda i: (0, 0)
    rows = lambda i: (i, 0)
    vec = pl.BlockSpec((1, RWKV_WIDTH), full)
    return pl.pallas_call(
        _rwkv_kernel,
        grid=(t // tb,),
        in_specs=[pl.BlockSpec((tb, RWKV_IN), rows),
                  pl.BlockSpec((1, RWKV_IN), full),
                  pl.BlockSpec((2 * LORA, 2 * RWKV_WIDTH), full),
                  vec, vec, vec, vec, vec, vec, vec],
        out_specs=pl.BlockSpec((tb, RWKV_WIDTH), rows),
        out_shape=jax.ShapeDtypeStruct((t, RWKV_WIDTH), BF16),
        scratch_shapes=[pltpu.VMEM((8, RWKV_IN), F32),
                        pltpu.VMEM((RWKV_WIDTH, RWKV_WIDTH), F32),
                        pltpu.VMEM((tb, RWKV_WIDTH), F32)],
        compiler_params=pltpu.CompilerParams(dimension_semantics=("arbitrary",),
                                             vmem_limit_bytes=VMEM_LIMIT),
        name="rwkv",
    )(u_rw, mu, w2a, w0, a0, k_k, k_a, r_k, lnx_w, lnx_b)


def _outproj_kernel(x_ref, ys_ref, yr_ref, ux_ref, kt_ref, vb_ref, ones_ref, w_ref, nw_ref, o_ref):
    q = ux_ref[:, :XATTN_WIDTH].astype(BF16)
    s = _dot(q, kt_ref[...])
    ps = []
    for h in range(XATTN_HEADS):
        sh = s[:, h * MEM_LEN:(h + 1) * MEM_LEN]
        ps.append(jnp.exp(sh - jnp.max(sh, axis=-1, keepdims=True)))
    p = jnp.concatenate(ps, axis=1).astype(BF16)
    o = _dot(p, vb_ref[...]) / _dot(p, ones_ref[...])
    o = o * _silu(ux_ref[:, XATTN_WIDTH:])
    y = jnp.concatenate([ys_ref[...], yr_ref[...], o.astype(BF16)], axis=1)
    d = _dot(y, w_ref[...])
    d = d * lax.rsqrt(jnp.mean(d * d, axis=-1, keepdims=True) + NORM_EPS) * nw_ref[...]
    o_ref[...] = x_ref[...] + d


def _outproj(x2d, y_ssm, y_rw, u_xa, kt_bd, v_bd, ones_bd, w_bf16, nw):
    t = x2d.shape[0]
    tb = min(TB_PROJ, t)
    full = lambda i: (0, 0)
    rows = lambda i: (i, 0)
    return pl.pallas_call(
        _outproj_kernel,
        grid=(t // tb,),
        in_specs=[pl.BlockSpec((tb, D_MODEL), rows),
                  pl.BlockSpec((tb, SSM_WIDTH), rows),
                  pl.BlockSpec((tb, RWKV_WIDTH), rows),
                  pl.BlockSpec((tb, XATTN_IN), rows),
                  pl.BlockSpec((XATTN_WIDTH, XATTN_HEADS * MEM_LEN), full),
                  pl.BlockSpec((XATTN_HEADS * MEM_LEN, XATTN_WIDTH), full),
                  pl.BlockSpec((XATTN_HEADS * MEM_LEN, XATTN_WIDTH), full),
                  pl.BlockSpec((D_MODEL, D_MODEL), full),
                  pl.BlockSpec((1, D_MODEL), full)],
        out_specs=pl.BlockSpec((tb, D_MODEL), rows),
        out_shape=jax.ShapeDtypeStruct((t, D_MODEL), F32),
        compiler_params=pltpu.CompilerParams(dimension_semantics=("parallel",),
                                             vmem_limit_bytes=VMEM_LIMIT),
        name="outproj",
    )(x2d, y_ssm, y_rw, u_xa, kt_bd, v_bd, ones_bd, w_bf16, nw)


def _block_diag_heads(blocks):
    h, r, c = blocks.shape
    eye = jnp.eye(h, dtype=blocks.dtype)
    return (eye[:, None, :, None] * blocks[:, :, None, :]).reshape(h * r, h * c)


def kernel(x, mem, mem_norm_w, w_mem_kv, pre_norm_w, w_in, conv_w, conv_b, dt_bias, a_log, d_skip,
           ssm_norm_w, shift_mu, w0, w2, a0, a2, k_k, k_a, r_k, lnx_w, lnx_b, w_out, post_norm_w):
    b, t, _ = x.shape
    assert b == 1 and mem.shape[0] == 1
    rep = lambda p: jnp.repeat(p, HEAD_DIM, axis=-1)

    kv = _memkv(mem[0], mem_norm_w[None, :], w_mem_kv.astype(BF16))
    mk = kv[:, :XATTN_WIDTH].reshape(MEM_LEN, XATTN_HEADS, HEAD_DIM)
    mv = kv[:, XATTN_WIDTH:].reshape(MEM_LEN, XATTN_HEADS, HEAD_DIM)
    kt_bd = _block_diag_heads(jnp.transpose(mk, (1, 2, 0)) * (HEAD_DIM ** -0.5)).astype(BF16)
    v_bd = _block_diag_heads(jnp.transpose(mv, (1, 0, 2))).astype(BF16)
    ones_bd = _block_diag_heads(jnp.ones((XATTN_HEADS, MEM_LEN, HEAD_DIM), BF16))

    w_dt = jnp.repeat(w_in[:, :, CONV_DIM + SSM_WIDTH:SSM_IN], HEAD_DIM, axis=-1)
    w_in_k = jnp.concatenate([w_in[:, :, :CONV_DIM + SSM_WIDTH], w_dt, w_in[:, :, SSM_IN:]],
                             axis=-1).astype(BF16)
    w_out_k = w_out.astype(BF16)
    zeros = jnp.zeros((DEPTH, LORA, RWKV_WIDTH), F32)
    w2a = jnp.concatenate([jnp.concatenate([w2, zeros], axis=-1),
                           jnp.concatenate([zeros, a2], axis=-1)], axis=1).astype(BF16)

    xc = x[0]
    for i in range(DEPTH):
        u_ssm, u_rw, u_xa = _inproj(xc, pre_norm_w[i][None, :], w_in_k[i])
        y_ssm = _ssd(u_ssm, conv_w[i], conv_b[i][None, :], rep(dt_bias[i])[None, :],
                     rep(a_log[i])[None, :], rep(d_skip[i])[None, :], ssm_norm_w[i][None, :])
        y_rw = _rwkv(u_rw, shift_mu[i][None, :], w2a[i], w0[i][None, :], a0[i][None, :],
                     k_k[i][None, :], k_a[i][None, :], r_k[i].reshape(1, RWKV_WIDTH),
                     lnx_w[i][None, :], lnx_b[i][None, :])
        xc = _outproj(xc, y_ssm, y_rw, u_xa, kt_bd, v_bd, ones_bd, w_out_k[i],
                      post_norm_w[i][None, :])
    return xc[None]
```

```python
import jax
import jax.numpy as jnp
from jax import lax
from jax.experimental import pallas as pl
from jax.experimental.pallas import tpu as pltpu

F32 = jnp.float32
BF16 = jnp.bfloat16

D_MODEL = 1024
DEPTH = 4
HEAD_DIM = 64
CHUNK = 64
MEM_LEN = 256
SSM_WIDTH = 512
SSM_HEADS = 8
SSM_GROUPS = 2
SSM_STATE = 128
CONV_WIDTH = 4
CONV_DIM = SSM_WIDTH + 2 * SSM_GROUPS * SSM_STATE
SSM_IN = CONV_DIM + SSM_WIDTH + SSM_HEADS
RWKV_WIDTH = 256
RWKV_HEADS = 4
LORA = 64
RWKV_IN = 4 * RWKV_WIDTH + 2 * LORA
XATTN_WIDTH = 256
XATTN_HEADS = 4
XATTN_IN = 2 * XATTN_WIDTH
NORM_EPS = 1e-6
LNX_EPS = 64e-5
L2_EPS = 1e-12

N_SSM = CONV_DIM + SSM_WIDTH + SSM_WIDTH
N_IN = N_SSM + RWKV_IN + XATTN_IN

TB = 512
CUMSUM_BLOCK = 256
VMEM_LIMIT = 56 * 1024 * 1024


def _dot(a, b):
    return jnp.dot(a, b, preferred_element_type=F32)


def _dot_nt(a, b):
    return lax.dot_general(a, b, (((1,), (1,)), ((), ())), preferred_element_type=F32)


def _dot_tn(a, b):
    return lax.dot_general(a, b, (((0,), (0,)), ((), ())), preferred_element_type=F32)


def _split3(x):
    hi = x.astype(BF16)
    r1 = x - hi.astype(F32)
    mid = r1.astype(BF16)
    lo = (r1 - mid.astype(F32)).astype(BF16)
    return hi, mid, lo


def _dot_exact_lhs(lhs_bf16, x):
    hi, mid, lo = _split3(x)
    return _dot(lhs_bf16, lo) + _dot(lhs_bf16, mid) + _dot(lhs_bf16, hi)


def _sigmoid(x):
    return 1.0 / (1.0 + jnp.exp(-x))


def _silu(x):
    return x * _sigmoid(x)


def _softplus(x):
    return jnp.maximum(x, 0.0) + jnp.log1p(jnp.exp(-jnp.abs(x)))


def _rms(x, eps):
    return x * lax.rsqrt(jnp.mean(x * x, axis=-1, keepdims=True) + eps)


def _iota(shape, dim):
    return lax.broadcasted_iota(jnp.int32, shape, dim)


def _chunk_cumsum(x):
    n = min(CUMSUM_BLOCK, x.shape[0])
    r = _iota((n, n), 0)
    c = _iota((n, n), 1)
    tril = jnp.where(((r >> 6) == (c >> 6)) & (c <= r), 1.0, 0.0).astype(BF16)
    return jnp.concatenate([_dot_exact_lhs(tril, x[i:i + n]) for i in range(0, x.shape[0], n)], axis=0)


def _shift_rows(x, carry, j):
    rolled = pltpu.roll(x, j, 0)
    croll = pltpu.roll(carry, j, 0)
    row = _iota(carry.shape, 0)
    top = jnp.where(row < j, croll, rolled[:8])
    return jnp.concatenate([top, rolled[8:]], axis=0)


def _chunks(tb):
    return [slice(c * CHUNK, (c + 1) * CHUNK) for c in range(tb // CHUNK)]


def _memkv_kernel(mem_ref, nw_ref, w_ref, o_ref):
    h = _rms(mem_ref[...], NORM_EPS) * nw_ref[...]
    o_ref[...] = _dot(h.astype(BF16), w_ref[...])


def _memkv(mem2d, nw, w_bf16):
    return pl.pallas_call(
        _memkv_kernel,
        out_shape=jax.ShapeDtypeStruct((MEM_LEN, 2 * XATTN_WIDTH), F32),
        name="memkv",
    )(mem2d, nw, w_bf16)


def _ssd_block(u, cw_ref, cb_ref, dtb_ref, alog_ref, dsk_ref, nw_ref, carry_ref, state_ref, y_ref):
    tb = u.shape[0]
    pre = u[:, :CONV_DIM]
    carry = carry_ref[...]
    conv = cb_ref[...] + pre * cw_ref[CONV_WIDTH - 1:CONV_WIDTH, :]
    for j in range(1, CONV_WIDTH):
        conv = conv + _shift_rows(pre, carry, j) * cw_ref[CONV_WIDTH - 1 - j:CONV_WIDTH - j, :]
    carry_ref[...] = pre[tb - 8:, :]
    xbc = _silu(conv)

    xs = xbc[:, :SSM_WIDTH]
    bmat = xbc[:, SSM_WIDTH:SSM_WIDTH + 256].astype(BF16)
    cmat = xbc[:, SSM_WIDTH + 256:].astype(BF16)
    dt = _softplus(u[:, CONV_DIM + SSM_WIDTH:] + dtb_ref[...])
    da = dt * (-jnp.exp(alog_ref[...]))
    xdt = xs * dt
    a_cs = _chunk_cumsum(da)
    ea = jnp.exp(a_cs)

    q_i = _iota((CHUNK, 128), 0)
    s_i = _iota((CHUNK, 128), 1) & (CHUNK - 1)
    eye2 = q_i == s_i
    causal2 = s_i <= q_i
    bdmask = (_iota((128, 128), 0) >> 6) == (_iota((128, 128), 1) >> 6)
    groups = [slice(g * 256, (g + 1) * 256) for g in range(SSM_GROUPS)]

    gps = [[_dot_nt(cmat[rows, g * 128:(g + 1) * 128],
                    jnp.concatenate([bmat[rows, g * 128:(g + 1) * 128]] * 2, axis=0))
            for g in range(SSM_GROUPS)] for rows in _chunks(tb)]

    y_diags, incs, decays = [], [], []
    for c, rows in enumerate(_chunks(tb)):
        acs_c = a_cs[rows]
        a_last = acs_c[CHUNK - 1:CHUNK, :]
        xdt_c = xdt[rows]
        xend_c = (xdt_c * jnp.exp(a_last - acs_c)).astype(BF16)
        decays.append(jnp.exp(a_last))
        parts = []
        for j in range(SSM_HEADS // 2):
            lanes = slice(j * 128, (j + 1) * 128)
            colp = acs_c[:, lanes]
            rowp = jnp.sum(jnp.where(eye2, colp, 0.0), axis=0, keepdims=True)
            dm = jnp.exp(jnp.where(causal2, colp - rowp, -1e30))
            m = (gps[c][j // 2] * dm).astype(BF16)
            xd = xdt_c[:, lanes]
            bd = jnp.where(bdmask, jnp.concatenate([xd, xd], axis=0), 0.0).astype(BF16)
            parts.append(_dot(m, bd))
        y_diags.append(jnp.concatenate(parts, axis=1))
        incs.append([_dot_tn(bmat[rows, g * 128:(g + 1) * 128], xend_c[:, groups[g]])
                     for g in range(SSM_GROUPS)])

    st = [state_ref[:, groups[g]] for g in range(SSM_GROUPS)]
    for c, rows in enumerate(_chunks(tb)):
        y_off = jnp.concatenate(
            [_dot(cmat[rows, g * 128:(g + 1) * 128], st[g].astype(BF16)) for g in range(SSM_GROUPS)], axis=1)
        y_ref[rows, :] = y_diags[c] + y_off * ea[rows]
        st = [st[g] * decays[c][:, groups[g]] + incs[c][g] for g in range(SSM_GROUPS)]
    for g in range(SSM_GROUPS):
        state_ref[:, groups[g]] = st[g]

    y = y_ref[...] + dsk_ref[...] * xs
    y = y * _silu(u[:, CONV_DIM:CONV_DIM + SSM_WIDTH])
    y = jnp.concatenate([_rms(y[:, groups[g]], NORM_EPS) for g in range(SSM_GROUPS)], axis=1)
    return (y * nw_ref[...]).astype(BF16)


def _rwkv_block(u, mu_ref, w2a_ref, w0_ref, a0_ref, kk_ref, ka_ref, rk_ref, lw_ref, lb_ref,
                carry_ref, state_ref, y_ref):
    tb = u.shape[0]
    W = RWKV_WIDTH
    prev = _shift_rows(u, carry_ref[...], 1)
    carry_ref[...] = u[tb - 8:, :]
    us = u + (prev - u) * mu_ref[...]
    r = us[:, :W]
    k = us[:, W:2 * W]
    v = us[:, 2 * W:3 * W]
    g = us[:, 3 * W:4 * W]
    lat = us[:, 4 * W:]
    lat = jnp.where(_iota(lat.shape, 1) < LORA, jnp.tanh(lat), lat)
    lora = _dot(lat.astype(BF16), w2a_ref[...])
    w_log = -_softplus(-(w0_ref[...] + lora[:, :W])) - 0.5
    logw = -jnp.exp(w_log)
    a = _sigmoid(a0_ref[...] + lora[:, W:])

    blk = (_iota((W, W), 0) >> 6) == (_iota((W, W), 1) >> 6)
    head_ones = jnp.where(blk, 1.0, 0.0).astype(BF16)
    head_avg = jnp.where(blk, 1.0 / HEAD_DIM, 0.0).astype(BF16)

    kkr = k * kk_ref[...]
    ss = _dot((kkr * kkr).astype(BF16), head_ones)
    kk = kkr / jnp.maximum(jnp.sqrt(ss), L2_EPS)
    k2 = k * (1.0 + (a - 1.0) * ka_ref[...])
    b_vec = kk * a

    cum = _chunk_cumsum(logw)
    e_c = jnp.exp(cum)
    e_n = jnp.exp(-cum)
    at = (-kk * jnp.exp(cum - logw)).astype(BF16)
    rt = r * e_c
    bt = (b_vec * e_n).astype(BF16)
    kt = (k2 * e_n).astype(BF16)
    vb = v.astype(BF16)

    t_i = _iota((CHUNK, W), 0)
    j_i = _iota((CHUNK, W), 1) & (CHUNK - 1)
    strict = j_i < t_i
    incl = j_i <= t_i
    head_masks_rows = [(_iota((CHUNK, W), 1) >> 6) == h for h in range(RWKV_HEADS)]
    eye_pair = jnp.where((_iota((CHUNK, 128), 1) & (CHUNK - 1)) == _iota((CHUNK, 128), 0), 1.0, 0.0)
    first_of_pair = (_iota((CHUNK, W), 1) & 127) < CHUNK
    zero_blk = jnp.zeros((CHUNK, W), BF16)

    def bd(x):
        return jnp.where(blk, jnp.concatenate([x] * RWKV_HEADS, axis=0), jnp.zeros((W, W), x.dtype))

    a_ak, a_rk, a_rb, p_pairs, t_pairs = [], [], [], [], []
    for rows in _chunks(tb):
        x_cat = jnp.concatenate([at[rows], rt[rows].astype(BF16)], axis=0)
        y_stack = jnp.concatenate(
            [jnp.where(m, y, jnp.zeros_like(y)) for y in (kt[rows], bt[rows]) for m in head_masks_rows],
            axis=0)
        aa = _dot_nt(x_cat, y_stack)
        a_ak.append(jnp.where(strict, aa[:CHUNK, :W], 0.0).astype(BF16))
        a_ab = jnp.where(strict, aa[:CHUNK, W:], 0.0)
        a_rk.append(jnp.where(incl, aa[CHUNK:, :W], 0.0).astype(BF16))
        a_rb.append(jnp.where(incl, aa[CHUNK:, W:], 0.0).astype(BF16))
        p_pairs.append([a_ab[:, :128].astype(BF16), a_ab[:, 128:].astype(BF16)])
        t_pairs.append([eye_pair, eye_pair])

    for step in range(6):
        for c in range(len(p_pairs)):
            for p in range(RWKV_HEADS // 2):
                pb = p_pairs[c][p]
                pt = jnp.concatenate([pb, t_pairs[c][p].astype(BF16)], axis=1)
                rhs = jnp.concatenate([jnp.where(first_of_pair, pt, jnp.zeros_like(pt)),
                                       jnp.where(first_of_pair, jnp.zeros_like(pt), pt)], axis=0)
                res = _dot(pb, rhs)
                p_pairs[c][p] = res[:, :128].astype(BF16)
                t_pairs[c][p] = t_pairs[c][p] + res[:, 128:]

    qs, y0s, ps, ns, decays = [], [], [], [], []
    for c, rows in enumerate(_chunks(tb)):
        cum_c = cum[rows]
        cum_last = cum_c[CHUNK - 1:CHUNK, :]
        e_end = jnp.exp(cum_last - cum_c)
        decays.append(jnp.exp(cum_last))
        v_bd = bd(vb[rows])
        av = _dot(a_ak[c], v_bd).astype(BF16)
        t_all = jnp.concatenate(t_pairs[c], axis=1).astype(BF16)
        wu = _dot(t_all, jnp.concatenate([bd(at[rows]), bd(av)], axis=1)).astype(BF16)
        qs.append((rt[rows] + _dot(a_rb[c], bd(wu[:, :W]))).astype(BF16))
        y0s.append(_dot(jnp.concatenate([a_rk[c], a_rb[c]], axis=1),
                        jnp.concatenate([v_bd, bd(wu[:, W:])], axis=0)))
        mn_lhs = jnp.concatenate([wu, jnp.concatenate([zero_blk, vb[rows]], axis=1)], axis=0)
        bk = jnp.concatenate([b_vec[rows] * e_end, k2[rows] * e_end], axis=0).astype(BF16)
        mn = _dot_tn(mn_lhs, bk)
        ps.append(jnp.where(blk, mn[:W], 0.0).astype(BF16))
        ns.append(jnp.where(blk, mn[W:], 0.0))

    s_bd = state_ref[...]
    for c, rows in enumerate(_chunks(tb)):
        sb = s_bd.astype(BF16)
        y_ref[rows, :] = _dot_nt(qs[c], sb) + y0s[c]
        s_bd = s_bd * decays[c] + _dot(sb, ps[c]) + ns[c]
    state_ref[...] = s_bd

    y = y_ref[...]
    yc = y - _dot(y.astype(BF16), head_avg)
    var = _dot((yc * yc).astype(BF16), head_avg)
    yn = yc * lax.rsqrt(var + LNX_EPS) * lw_ref[...] + lb_ref[...]
    bonus = _dot((r * k2 * rk_ref[...]).astype(BF16), head_ones) * v
    return ((yn + bonus) * _silu(g)).astype(BF16)


def _xattn_block(u, kt_ref, vb_ref, ones_ref):
    s = _dot(u[:, :XATTN_WIDTH].astype(BF16), kt_ref[...])
    ps = []
    for h in range(XATTN_HEADS):
        sh = s[:, h * MEM_LEN:(h + 1) * MEM_LEN]
        ps.append(jnp.exp(sh - jnp.max(sh, axis=-1, keepdims=True)))
    p = jnp.concatenate(ps, axis=1).astype(BF16)
    o = _dot(p, vb_ref[...]) / _dot(p, ones_ref[...])
    return (o * _silu(u[:, XATTN_WIDTH:])).astype(BF16)


def _layer_kernel(x_ref, prew_ref, win_ref,
                  cw_ref, cb_ref, dtb_ref, alog_ref, dsk_ref, snw_ref,
                  mu_ref, w2a_ref, w0_ref, a0_ref, kk_ref, ka_ref, rk_ref, lw_ref, lb_ref,
                  kt_ref, vb_ref, ones_ref, wout_ref, postw_ref,
                  o_ref,
                  ssd_carry, ssd_state, ssd_y, rw_carry, rw_state, rw_y):
    @pl.when(pl.program_id(0) == 0)
    def _():
        for ref in (ssd_carry, ssd_state, rw_carry, rw_state):
            ref[...] = jnp.zeros_like(ref)

    x = x_ref[...]
    hb = (_rms(x, NORM_EPS) * prew_ref[...]).astype(BF16)
    u_rw = _dot(hb, win_ref[:, N_SSM:N_SSM + RWKV_IN])
    u_ssm = _dot(hb, win_ref[:, :N_SSM])
    u_xa = _dot(hb, win_ref[:, N_SSM + RWKV_IN:])
    y_rw = _rwkv_block(u_rw, mu_ref, w2a_ref, w0_ref, a0_ref, kk_ref, ka_ref, rk_ref, lw_ref, lb_ref,
                       rw_carry, rw_state, rw_y)
    y_ssm = _ssd_block(u_ssm, cw_ref, cb_ref, dtb_ref, alog_ref, dsk_ref, snw_ref,
                       ssd_carry, ssd_state, ssd_y)
    y_mem = _xattn_block(u_xa, kt_ref, vb_ref, ones_ref)
    d = _dot(jnp.concatenate([y_ssm, y_rw, y_mem], axis=1), wout_ref[...])
    o_ref[...] = x + _rms(d, NORM_EPS) * postw_ref[...]


def _layer(x2d, params):
    t = x2d.shape[0]
    tb = min(TB, t)
    assert t % tb == 0 and tb % CHUNK == 0
    rows = lambda i: (i, 0)
    whole = lambda a: pl.BlockSpec(a.shape, lambda i: (0,) * a.ndim)
    return pl.pallas_call(
        _layer_kernel,
        grid=(t // tb,),
        in_specs=[pl.BlockSpec((tb, D_MODEL), rows)] + [whole(p) for p in params],
        out_specs=pl.BlockSpec((tb, D_MODEL), rows),
        out_shape=jax.ShapeDtypeStruct((t, D_MODEL), F32),
        scratch_shapes=[pltpu.VMEM((8, CONV_DIM), F32),
                        pltpu.VMEM((SSM_STATE, SSM_WIDTH), F32),
                        pltpu.VMEM((tb, SSM_WIDTH), F32),
                        pltpu.VMEM((8, RWKV_IN), F32),
                        pltpu.VMEM((RWKV_WIDTH, RWKV_WIDTH), F32),
                        pltpu.VMEM((tb, RWKV_WIDTH), F32)],
        compiler_params=pltpu.CompilerParams(dimension_semantics=("arbitrary",),
                                             vmem_limit_bytes=VMEM_LIMIT),
        name="layer",
    )(x2d, *params)


def _block_diag_heads(blocks):
    h, r, c = blocks.shape
    eye = jnp.eye(h, dtype=blocks.dtype)
    return (eye[:, None, :, None] * blocks[:, :, None, :]).reshape(h * r, h * c)


def kernel(x, mem, mem_norm_w, w_mem_kv, pre_norm_w, w_in, conv_w, conv_b, dt_bias, a_log, d_skip,
           ssm_norm_w, shift_mu, w0, w2, a0, a2, k_k, k_a, r_k, lnx_w, lnx_b, w_out, post_norm_w):
    assert x.shape[0] == 1 and mem.shape[0] == 1
    rep = lambda p: jnp.repeat(p, HEAD_DIM, axis=-1)
    row = lambda p: p.reshape(1, -1)

    kv = _memkv(mem[0], row(mem_norm_w), w_mem_kv.astype(BF16))
    mk = kv[:, :XATTN_WIDTH].reshape(MEM_LEN, XATTN_HEADS, HEAD_DIM)
    mv = kv[:, XATTN_WIDTH:].reshape(MEM_LEN, XATTN_HEADS, HEAD_DIM)
    kt_bd = _block_diag_heads(jnp.transpose(mk, (1, 2, 0)) * (HEAD_DIM ** -0.5)).astype(BF16)
    v_bd = _block_diag_heads(jnp.transpose(mv, (1, 0, 2))).astype(BF16)
    ones_bd = _block_diag_heads(jnp.ones((XATTN_HEADS, MEM_LEN, HEAD_DIM), BF16))

    w_dt = jnp.repeat(w_in[:, :, CONV_DIM + SSM_WIDTH:SSM_IN], HEAD_DIM, axis=-1)
    w_in_k = jnp.concatenate([w_in[:, :, :CONV_DIM + SSM_WIDTH], w_dt, w_in[:, :, SSM_IN:]],
                             axis=-1).astype(BF16)
    w_out_k = w_out.astype(BF16)
    zeros = jnp.zeros((DEPTH, LORA, RWKV_WIDTH), F32)
    w2a = jnp.concatenate([jnp.concatenate([w2, zeros], axis=-1),
                           jnp.concatenate([zeros, a2], axis=-1)], axis=1).astype(BF16)

    xc = x[0]
    for i in range(DEPTH):
        params = (row(pre_norm_w[i]), w_in_k[i],
                  conv_w[i], row(conv_b[i]), row(rep(dt_bias[i])), row(rep(a_log[i])),
                  row(rep(d_skip[i])), row(ssm_norm_w[i]),
                  row(shift_mu[i]), w2a[i], row(w0[i]), row(a0[i]), row(k_k[i]), row(k_a[i]),
                  row(r_k[i]), row(lnx_w[i]), row(lnx_b[i]),
                  kt_bd, v_bd, ones_bd, w_out_k[i], row(post_norm_w[i]))
        xc = _layer(xc, params)
    return xc[None]
```

```python
import functools

import jax
import jax.numpy as jnp
from jax import lax
from jax.experimental import pallas as pl
from jax.experimental.pallas import tpu as pltpu

F32 = jnp.float32
BF16 = jnp.bfloat16

D_MODEL = 1024
DEPTH = 4
HEAD_DIM = 64
CHUNK = 64
MEM_LEN = 256
SSM_WIDTH = 512
SSM_HEADS = 8
SSM_GROUPS = 2
SSM_STATE = 128
CONV_WIDTH = 4
CONV_DIM = SSM_WIDTH + 2 * SSM_GROUPS * SSM_STATE
SSM_IN = CONV_DIM + SSM_WIDTH + SSM_HEADS
RWKV_WIDTH = 256
RWKV_HEADS = 4
LORA = 64
RWKV_IN = 4 * RWKV_WIDTH + 2 * LORA
XATTN_WIDTH = 256
XATTN_HEADS = 4
XATTN_IN = 2 * XATTN_WIDTH
NORM_EPS = 1e-6
LNX_EPS = 64e-5
L2_EPS = 1e-12

TB = 512
CUMSUM_BLOCK = 256
PIECE = 256
VMEM_LIMIT = 56 * 1024 * 1024


_dot = functools.partial(jnp.dot, preferred_element_type=F32)
_dot_nt = functools.partial(lax.dot_general, dimension_numbers=(((1,), (1,)), ((), ())),
                            preferred_element_type=F32)
_dot_tn = functools.partial(lax.dot_general, dimension_numbers=(((0,), (0,)), ((), ())),
                            preferred_element_type=F32)


def _dot_split_rhs(lhs_bf16, x):
    hi = x.astype(BF16)
    lo = (x - hi.astype(F32)).astype(BF16)
    return _dot(lhs_bf16, lo) + _dot(lhs_bf16, hi)


def _sigmoid(x):
    return 1.0 / (1.0 + jnp.exp(-x))


def _silu(x):
    return x * _sigmoid(x)


def _softplus(x):
    return jnp.maximum(x, 0.0) + jnp.log(1.0 + jnp.exp(-jnp.abs(x)))


def _rms(x, eps):
    return x * lax.rsqrt(jnp.mean(x * x, axis=-1, keepdims=True) + eps)


def _iota(shape, dim):
    return lax.broadcasted_iota(jnp.int32, shape, dim)


def _chunk_cumsum(slabs):
    n = CHUNK * len(slabs)
    r = _iota((n, n), 0)
    c = _iota((n, n), 1)
    tril = jnp.where(((r >> 6) == (c >> 6)) & (c <= r), 1.0, 0.0).astype(BF16)
    out = _dot_split_rhs(tril, jnp.concatenate(slabs, axis=0))
    return [out[i * CHUNK:(i + 1) * CHUNK] for i in range(len(slabs))]


def _chunks(tb):
    return [slice(c * CHUNK, (c + 1) * CHUNK) for c in range(tb // CHUNK)]


def _keep_tail(buf_ref, next_ref):
    tb = buf_ref.shape[0] - 8
    next_ref[0:8, :] = buf_ref[tb:tb + 8, :]


def _memkv_kernel(mem_ref, nw_ref, w_ref, o_ref):
    h = _rms(mem_ref[...], NORM_EPS) * nw_ref[...]
    o_ref[...] = _dot(h.astype(BF16), w_ref[...])


def _memkv(mem2d, nw, w_bf16):
    return pl.pallas_call(
        _memkv_kernel,
        out_shape=jax.ShapeDtypeStruct((MEM_LEN, 2 * XATTN_WIDTH), F32),
        name="memkv",
    )(mem2d, nw, w_bf16)


def _ssd_prep(pre_ref, zdt_ref, rows, cw_ref, cb_ref, dtb_ref, alog_ref):
    lo = 8 + rows.start
    conv = cb_ref[...] + pre_ref[lo:lo + CHUNK, :] * cw_ref[CONV_WIDTH - 1:CONV_WIDTH, :]
    for j in range(1, CONV_WIDTH):
        conv = conv + pre_ref[lo - j:lo - j + CHUNK, :] * cw_ref[CONV_WIDTH - 1 - j:CONV_WIDTH - j, :]
    xbc = _silu(conv)
    xs = xbc[:, :SSM_WIDTH]
    dt = _softplus(zdt_ref[rows, SSM_WIDTH:] + dtb_ref[...])
    return dict(xs=xs,
                bm=xbc[:, SSM_WIDTH:SSM_WIDTH + 256].astype(BF16),
                cm=xbc[:, SSM_WIDTH + 256:].astype(BF16),
                da=dt * (-jnp.exp(alog_ref[...])),
                xdt=xs * dt,
                zg=_silu(zdt_ref[rows, :SSM_WIDTH]))


_SSD_GROUPS = [slice(g * 256, (g + 1) * 256) for g in range(SSM_GROUPS)]
_SSD_GROUP_STATE = [slice(g * SSM_STATE, (g + 1) * SSM_STATE) for g in range(SSM_GROUPS)]


def _ssd_intra(p, acs_c):
    q_i = _iota((CHUNK, 128), 0)
    s_i = _iota((CHUNK, 128), 1) & (CHUNK - 1)
    eye2 = q_i == s_i
    causal2 = s_i <= q_i
    bdmask = (_iota((128, 128), 0) >> 6) == (_iota((128, 128), 1) >> 6)
    gl = _SSD_GROUP_STATE
    gps = [_dot_nt(p["cm"][:, gl[g]], jnp.concatenate([p["bm"][:, gl[g]]] * 2, axis=0))
           for g in range(SSM_GROUPS)]
    a_last = acs_c[CHUNK - 1:CHUNK, :]
    xend_c = (p["xdt"] * jnp.exp(a_last - acs_c)).astype(BF16)
    parts = []
    for j in range(SSM_HEADS // 2):
        lanes = slice(j * 128, (j + 1) * 128)
        colp = acs_c[:, lanes]
        rowp = jnp.sum(jnp.where(eye2, colp, 0.0), axis=0, keepdims=True)
        dm = jnp.exp(jnp.where(causal2, colp - rowp, -1e30))
        m = (gps[j // 2] * dm).astype(BF16)
        xd = p["xdt"][:, lanes]
        bd = jnp.where(bdmask, jnp.concatenate([xd, xd], axis=0), 0.0).astype(BF16)
        parts.append(_dot(m, bd))
    incs = [_dot_tn(p["bm"][:, gl[g]], xend_c[:, _SSD_GROUPS[g]]) for g in range(SSM_GROUPS)]
    return jnp.concatenate(parts, axis=1), incs, jnp.exp(a_last)


def _ssd_state_stage(prep, a_cs, intra, dsk_ref, nw_ref, state_ref, y_ref):
    groups, gl = _SSD_GROUPS, _SSD_GROUP_STATE
    st = [state_ref[:, groups[g]] for g in range(SSM_GROUPS)]
    for c, p in enumerate(prep):
        y_diag, incs, decay = intra[c]
        y_off = jnp.concatenate([_dot(p["cm"][:, gl[g]], st[g].astype(BF16)) for g in range(SSM_GROUPS)],
                                axis=1)
        y = (y_diag + y_off * jnp.exp(a_cs[c]) + dsk_ref[...] * p["xs"]) * p["zg"]
        y_ref[c * CHUNK:(c + 1) * CHUNK, :] = y
        st = [st[g] * decay[:, groups[g]] + incs[g] for g in range(SSM_GROUPS)]
    for g in range(SSM_GROUPS):
        state_ref[:, groups[g]] = st[g]

    y = y_ref[...]
    y = jnp.concatenate([_rms(y[:, groups[g]], NORM_EPS) for g in range(SSM_GROUPS)], axis=1)
    return (y * nw_ref[...]).astype(BF16)


def _head_block_mask():
    w = RWKV_WIDTH
    return (_iota((w, w), 0) >> 6) == (_iota((w, w), 1) >> 6)


def _rwkv_prep(u_ref, rows, mu_ref, w2a_ref, w0_ref, a0_ref, kk_ref, ka_ref):
    W = RWKV_WIDTH
    lo = 8 + rows.start
    u = u_ref[lo:lo + CHUNK, :]
    prev = u_ref[lo - 1:lo - 1 + CHUNK, :]
    us = u + (prev - u) * mu_ref[...]
    k = us[:, W:2 * W]
    lat = us[:, 4 * W:]
    lat = jnp.where(_iota(lat.shape, 1) < LORA, jnp.tanh(lat), lat)
    lora = _dot(lat.astype(BF16), w2a_ref[...])
    w_log = -_softplus(-(w0_ref[...] + lora[:, :W])) - 0.5
    a = _sigmoid(a0_ref[...] + lora[:, W:])
    head_ones = jnp.where(_head_block_mask(), 1.0, 0.0).astype(BF16)
    kkr = k * kk_ref[...]
    kk = kkr / jnp.maximum(jnp.sqrt(_dot((kkr * kkr).astype(BF16), head_ones)), L2_EPS)
    return dict(r=us[:, :W], v=us[:, 2 * W:3 * W], g=us[:, 3 * W:4 * W],
                logw=-jnp.exp(w_log),
                kk=kk, k2=k * (1.0 + (a - 1.0) * ka_ref[...]), b=kk * a)


def _rwkv_stages(prep, cum, rk_ref, lw_ref, lb_ref, state_ref, y_ref, run_pieces, side_work):
    W = RWKV_WIDTH
    blk = _head_block_mask()
    head_ones = jnp.where(blk, 1.0, 0.0).astype(BF16)
    head_avg = jnp.where(blk, 1.0 / HEAD_DIM, 0.0).astype(BF16)
    t_i = _iota((CHUNK, W), 0)
    j_i = _iota((CHUNK, W), 1) & (CHUNK - 1)
    strict = j_i < t_i
    incl = j_i <= t_i
    head_masks_rows = [(_iota((CHUNK, W), 1) >> 6) == h for h in range(RWKV_HEADS)]
    eye_pair = jnp.where((_iota((CHUNK, 128), 1) & (CHUNK - 1)) == _iota((CHUNK, 128), 0), 1.0, 0.0)
    first_of_pair = (_iota((CHUNK, W), 1) & 127) < CHUNK
    zero_blk = jnp.zeros((CHUNK, W), BF16)

    def bd(x):
        return jnp.where(blk, jnp.concatenate([x] * RWKV_HEADS, axis=0), jnp.zeros((W, W), x.dtype))

    ats, rts, vbs, a_ak, a_rk, a_rb, p_pairs, t_pairs = [], [], [], [], [], [], [], []
    for p, cum_c in zip(prep, cum):
        e_n = jnp.exp(-cum_c)
        at = (-p["kk"] * jnp.exp(cum_c - p["logw"])).astype(BF16)
        rt = p["r"] * jnp.exp(cum_c)
        kt = (p["k2"] * e_n).astype(BF16)
        bt = (p["b"] * e_n).astype(BF16)
        ats.append(at)
        rts.append(rt)
        vbs.append(p["v"].astype(BF16))
        x_cat = jnp.concatenate([at, rt.astype(BF16)], axis=0)
        y_stack = jnp.concatenate(
            [jnp.where(m, y, jnp.zeros_like(y)) for y in (kt, bt) for m in head_masks_rows], axis=0)
        aa = _dot_nt(x_cat, y_stack)
        a_ak.append(jnp.where(strict, aa[:CHUNK, :W], 0.0).astype(BF16))
        a_ab = jnp.where(strict, aa[:CHUNK, W:], 0.0)
        a_rk.append(jnp.where(incl, aa[CHUNK:, :W], 0.0).astype(BF16))
        a_rb.append(jnp.where(incl, aa[CHUNK:, W:], 0.0).astype(BF16))
        p_pairs.append([a_ab[:, :128].astype(BF16), a_ab[:, 128:].astype(BF16)])
        t_pairs.append([eye_pair, eye_pair])

    for step in range(6):
        for c in range(len(p_pairs)):
            for h2 in range(RWKV_HEADS // 2):
                pb = p_pairs[c][h2]
                pt = jnp.concatenate([pb, t_pairs[c][h2].astype(BF16)], axis=1)
                rhs = jnp.concatenate([jnp.where(first_of_pair, pt, jnp.zeros_like(pt)),
                                       jnp.where(first_of_pair, jnp.zeros_like(pt), pt)], axis=0)
                res = _dot(pb, rhs)
                p_pairs[c][h2] = res[:, :128].astype(BF16)
                t_pairs[c][h2] = t_pairs[c][h2] + res[:, 128:]
        run_pieces(2 if step == 0 else 1, t_pairs[0][0])
        side_work(step)

    nc = len(prep)
    v_bds = [bd(vbs[c]) for c in range(nc)]
    avs = [_dot(a_ak[c], v_bds[c]).astype(BF16) for c in range(nc)]
    wus = [_dot(jnp.concatenate(t_pairs[c], axis=1).astype(BF16),
                jnp.concatenate([bd(ats[c]), bd(avs[c])], axis=1)).astype(BF16) for c in range(nc)]
    qs = [(rts[c] + _dot(a_rb[c], bd(wus[c][:, :W]))).astype(BF16) for c in range(nc)]
    y0s = [_dot(jnp.concatenate([a_rk[c], a_rb[c]], axis=1),
                jnp.concatenate([v_bds[c], bd(wus[c][:, W:])], axis=0)) for c in range(nc)]
    ps, ns, decays = [], [], []
    for c, (p, cum_c) in enumerate(zip(prep, cum)):
        cum_last = cum_c[CHUNK - 1:CHUNK, :]
        e_end = jnp.exp(cum_last - cum_c)
        decays.append(jnp.exp(cum_last))
        mn_lhs = jnp.concatenate([wus[c], jnp.concatenate([zero_blk, vbs[c]], axis=1)], axis=0)
        bk = jnp.concatenate([p["b"] * e_end, p["k2"] * e_end], axis=0).astype(BF16)
        mn = _dot_tn(mn_lhs, bk)
        ps.append(jnp.where(blk, mn[:W], 0.0).astype(BF16))
        ns.append(jnp.where(blk, mn[W:], 0.0))

    s_bd = state_ref[...]
    for c in range(len(prep)):
        sb = s_bd.astype(BF16)
        y_ref[c * CHUNK:(c + 1) * CHUNK, :] = _dot_nt(qs[c], sb) + y0s[c]
        s_bd = s_bd * decays[c] + _dot(sb, ps[c]) + ns[c]
    state_ref[...] = s_bd

    whole = lambda key: jnp.concatenate([p[key] for p in prep], axis=0)
    r, k2, v, g = whole("r"), whole("k2"), whole("v"), whole("g")
    y = y_ref[...]
    yc = y - _dot(y.astype(BF16), head_avg)
    var = _dot((yc * yc).astype(BF16), head_avg)
    yn = yc * lax.rsqrt(var + LNX_EPS) * lw_ref[...] + lb_ref[...]
    bonus = _dot((r * k2 * rk_ref[...]).astype(BF16), head_ones) * v
    return ((yn + bonus) * _silu(g)).astype(BF16)


def _xattn_block(u, kt_ref, vb_ref, ones_ref):
    s = _dot(u[:, :XATTN_WIDTH].astype(BF16), kt_ref[...])
    ps = []
    for h in range(XATTN_HEADS):
        sh = s[:, h * MEM_LEN:(h + 1) * MEM_LEN]
        ps.append(jnp.exp(sh - jnp.max(sh, axis=-1, keepdims=True)))
    p = jnp.concatenate(ps, axis=1).astype(BF16)
    o = _dot(p, vb_ref[...]) / _dot(p, ones_ref[...])
    return (o * _silu(u[:, XATTN_WIDTH:])).astype(BF16)


def _inproj_pieces(x, prew_ref, wmain_ref, wdt_ref, wrest_ref, rw_u, ssd_pre, zdt, xa):
    tb = x.shape[0]
    hb = (_rms(x, NORM_EPS) * prew_ref[...]).astype(BF16)
    pieces = []
    for dst, row0, dcol, w_ref, wcol, width in (
            (rw_u, 8, 0, wrest_ref, 0, RWKV_IN),
            (ssd_pre, 8, 0, wmain_ref, 0, CONV_DIM),
            (zdt, 0, 0, wmain_ref, CONV_DIM, SSM_WIDTH),
            (zdt, 0, SSM_WIDTH, wdt_ref, 0, SSM_WIDTH),
            (xa, 0, 0, wrest_ref, RWKV_IN, XATTN_IN)):
        for c in range(0, width, PIECE):
            w = min(PIECE, width - c)

            def piece(anchor=None, dst=dst, row0=row0, d0=dcol + c, w_ref=w_ref, w0=wcol + c, w=w):
                lhs = hb if anchor is None else _after(hb, anchor)
                dst[row0:row0 + tb, d0:d0 + w] = _dot(lhs, w_ref[:, w0:w0 + w])
            pieces.append(piece)
    return pieces


def _after(x_bf16, anchor):
    bits = pltpu.bitcast(anchor[:16, :128], jnp.int32)
    zero = lax.shift_right_logical(lax.shift_right_logical(bits, 16), 16)
    zero = jnp.concatenate([zero.astype(F32).astype(BF16)] * (x_bf16.shape[1] // 128), axis=1)
    return jnp.concatenate([x_bf16[:16] + zero, x_bf16[16:]], axis=0)


def _layer_kernel(x_ref, xnext_ref, prew_ref, wmain_ref, wdt_ref, wrest_ref,
                  cw_ref, cb_ref, dtb_ref, alog_ref, dsk_ref, snw_ref,
                  mu_ref, w2a_ref, w0_ref, a0_ref, kk_ref, ka_ref, rk_ref, lw_ref, lb_ref,
                  kt_ref, vb_ref, ones_ref, wout_ref, postw_ref,
                  o_ref,
                  rw_u0, ssd_pre0, zdt0, xa0, rw_u1, ssd_pre1, zdt1, xa1,
                  ssd_state, ssd_y, rw_state, rw_y):
    step = pl.program_id(0)
    tb = x_ref.shape[0]
    bufs = ((rw_u0, ssd_pre0, zdt0, xa0), (rw_u1, ssd_pre1, zdt1, xa1))
    weights = (prew_ref, wmain_ref, wdt_ref, wrest_ref)

    @pl.when(step == 0)
    def _():
        rw_u0[0:8, :] = jnp.zeros((8, RWKV_IN), F32)
        ssd_pre0[0:8, :] = jnp.zeros((8, CONV_DIM), F32)
        ssd_state[...] = jnp.zeros_like(ssd_state)
        rw_state[...] = jnp.zeros_like(rw_state)
        for piece in _inproj_pieces(x_ref[...], *weights, *bufs[0]):
            piece()

    def body(cur, nxt):
        pieces = _inproj_pieces(xnext_ref[...], *weights, *nxt)
        _keep_tail(cur[0], nxt[0])
        _keep_tail(cur[1], nxt[1])
        rw_prep, sd_prep, cum, a_cs = [], [], [], []

        def run_pieces(count, anchor):
            for _ in range(min(count, len(pieces))):
                pieces.pop(0)(anchor)

        for rows in _chunks(tb):
            rw_prep.append(_rwkv_prep(cur[0], rows, mu_ref, w2a_ref, w0_ref, a0_ref, kk_ref, ka_ref))
            sd_prep.append(_ssd_prep(cur[1], cur[2], rows, cw_ref, cb_ref, dtb_ref, alog_ref))
            run_pieces(1, rw_prep[-1]["kk"])
            done = len(rw_prep)
            if (done * CHUNK) % CUMSUM_BLOCK == 0 or done * CHUNK == tb:
                cum += _chunk_cumsum([p["logw"] for p in rw_prep[len(cum):]])
                a_cs += _chunk_cumsum([p["da"] for p in sd_prep[len(a_cs):]])
        intra = []

        def ssd_side_work(step, steps=6):
            upto = -(-(step + 1) * len(sd_prep) // steps)
            while len(intra) < upto:
                intra.append(_ssd_intra(sd_prep[len(intra)], a_cs[len(intra)]))

        y_rw = _rwkv_stages(rw_prep, cum, rk_ref, lw_ref, lb_ref, rw_state, rw_y, run_pieces, ssd_side_work)
        assert not pieces and len(intra) == len(sd_prep)
        y_ssm = _ssd_state_stage(sd_prep, a_cs, intra, dsk_ref, snw_ref, ssd_state, ssd_y)
        y_mem = _xattn_block(cur[3][...], kt_ref, vb_ref, ones_ref)
        d = _dot(jnp.concatenate([y_ssm, y_rw, y_mem], axis=1), wout_ref[...])
        o_ref[...] = x_ref[...] + _rms(d, NORM_EPS) * postw_ref[...]

    parity = lax.rem(step, 2)
    pl.when(parity == 0)(lambda: body(bufs[0], bufs[1]))
    pl.when(parity == 1)(lambda: body(bufs[1], bufs[0]))


def _layer(x2d, params):
    t = x2d.shape[0]
    tb = min(TB, t)
    assert t % tb == 0 and tb % CHUNK == 0
    n = t // tb
    rows = lambda i: (i, 0)
    next_rows = lambda i: (jnp.minimum(i + 1, n - 1), 0)
    whole = lambda a: pl.BlockSpec(a.shape, lambda i: (0,) * a.ndim)
    return pl.pallas_call(
        _layer_kernel,
        grid=(n,),
        in_specs=[pl.BlockSpec((tb, D_MODEL), rows), pl.BlockSpec((tb, D_MODEL), next_rows)]
        + [whole(p) for p in params],
        out_specs=pl.BlockSpec((tb, D_MODEL), rows),
        out_shape=jax.ShapeDtypeStruct((t, D_MODEL), F32),
        scratch_shapes=2 * [pltpu.VMEM((tb + 8, RWKV_IN), F32),
                            pltpu.VMEM((tb + 8, CONV_DIM), F32),
                            pltpu.VMEM((tb, 2 * SSM_WIDTH), F32),
                            pltpu.VMEM((tb, XATTN_IN), F32)]
        + [pltpu.VMEM((SSM_STATE, SSM_WIDTH), F32),
           pltpu.VMEM((tb, SSM_WIDTH), F32),
           pltpu.VMEM((RWKV_WIDTH, RWKV_WIDTH), F32),
           pltpu.VMEM((tb, RWKV_WIDTH), F32)],
        compiler_params=pltpu.CompilerParams(dimension_semantics=("arbitrary",),
                                             vmem_limit_bytes=VMEM_LIMIT),
        name="layer",
    )(x2d, x2d, *params)


def _block_diag_heads(blocks):
    h, r, c = blocks.shape
    eye = jnp.eye(h, dtype=blocks.dtype)
    return (eye[:, None, :, None] * blocks[:, :, None, :]).reshape(h * r, h * c)


def kernel(x, mem, mem_norm_w, w_mem_kv, pre_norm_w, w_in, conv_w, conv_b, dt_bias, a_log, d_skip,
           ssm_norm_w, shift_mu, w0, w2, a0, a2, k_k, k_a, r_k, lnx_w, lnx_b, w_out, post_norm_w):
    assert x.shape[0] == 1 and mem.shape[0] == 1
    rep = lambda p: jnp.repeat(p, HEAD_DIM, axis=-1)
    row = lambda p: p.reshape(1, -1)

    kv = _memkv(mem[0], row(mem_norm_w), w_mem_kv.astype(BF16))
    mk = kv[:, :XATTN_WIDTH].reshape(MEM_LEN, XATTN_HEADS, HEAD_DIM)
    mv = kv[:, XATTN_WIDTH:].reshape(MEM_LEN, XATTN_HEADS, HEAD_DIM)
    kt_bd = _block_diag_heads(jnp.transpose(mk, (1, 2, 0)) * (HEAD_DIM ** -0.5)).astype(BF16)
    v_bd = _block_diag_heads(jnp.transpose(mv, (1, 0, 2))).astype(BF16)
    ones_bd = _block_diag_heads(jnp.ones((XATTN_HEADS, MEM_LEN, HEAD_DIM), BF16))

    w_main = w_in[:, :, :CONV_DIM + SSM_WIDTH].astype(BF16)
    w_dt = jnp.repeat(w_in[:, :, CONV_DIM + SSM_WIDTH:SSM_IN].astype(BF16), HEAD_DIM, axis=-1)
    w_rest = w_in[:, :, SSM_IN:].astype(BF16)
    w_out_k = w_out.astype(BF16)
    zeros = jnp.zeros((DEPTH, LORA, RWKV_WIDTH), F32)
    w2a = jnp.concatenate([jnp.concatenate([w2, zeros], axis=-1),
                           jnp.concatenate([zeros, a2], axis=-1)], axis=1).astype(BF16)

    xc = x[0]
    for i in range(DEPTH):
        params = (row(pre_norm_w[i]), w_main[i], w_dt[i], w_rest[i],
                  conv_w[i], row(conv_b[i]), row(rep(dt_bias[i])), row(rep(a_log[i])),
                  row(rep(d_skip[i])), row(ssm_norm_w[i]),
                  row(shift_mu[i]), w2a[i], row(w0[i]), row(a0[i]), row(k_k[i]), row(k_a[i]),
                  row(r_k[i]), row(lnx_w[i]), row(lnx_b[i]),
                  kt_bd, v_bd, ones_bd, w_out_k[i], row(post_norm_w[i]))
        xc = _layer(xc, params)
    return xc[None]
```

```python
import functools

import jax
import jax.numpy as jnp
from jax import lax
from jax.experimental import pallas as pl
from jax.experimental.pallas import tpu as pltpu

F32 = jnp.float32
BF16 = jnp.bfloat16

D_MODEL = 1024
DEPTH = 4
HEAD_DIM = 64
CHUNK = 64
MEM_LEN = 256
SSM_WIDTH = 512
SSM_HEADS = 8
SSM_GROUPS = 2
SSM_STATE = 128
CONV_WIDTH = 4
CONV_DIM = SSM_WIDTH + 2 * SSM_GROUPS * SSM_STATE
SSM_IN = CONV_DIM + SSM_WIDTH + SSM_HEADS
RWKV_WIDTH = 256
RWKV_HEADS = 4
LORA = 64
RWKV_IN = 4 * RWKV_WIDTH + 2 * LORA
XATTN_WIDTH = 256
XATTN_HEADS = 4
XATTN_IN = 2 * XATTN_WIDTH
NORM_EPS = 1e-6
LNX_EPS = 64e-5
L2_EPS = 1e-12

TB = 512
CUMSUM_BLOCK = 256
PIECE = 256
VMEM_LIMIT = 56 * 1024 * 1024


_dot = functools.partial(jnp.dot, preferred_element_type=F32)
_dot_nt = functools.partial(lax.dot_general, dimension_numbers=(((1,), (1,)), ((), ())),
                            preferred_element_type=F32)
_dot_tn = functools.partial(lax.dot_general, dimension_numbers=(((0,), (0,)), ((), ())),
                            preferred_element_type=F32)


def _dot_split_rhs(lhs_bf16, x):
    hi = x.astype(BF16)
    lo = (x - hi.astype(F32)).astype(BF16)
    return _dot(lhs_bf16, lo) + _dot(lhs_bf16, hi)


def _sigmoid(x):
    return 1.0 / (1.0 + jnp.exp(-x))


def _silu(x):
    return x * _sigmoid(x)


def _softplus(x):
    return jnp.maximum(x, 0.0) + jnp.log(1.0 + jnp.exp(-jnp.abs(x)))


def _rms(x, eps):
    return x * lax.rsqrt(jnp.mean(x * x, axis=-1, keepdims=True) + eps)


def _iota(shape, dim):
    return lax.broadcasted_iota(jnp.int32, shape, dim)


def _chunk_cumsum(slabs):
    n = CHUNK * len(slabs)
    r = _iota((n, n), 0)
    c = _iota((n, n), 1)
    tril = jnp.where(((r >> 6) == (c >> 6)) & (c <= r), 1.0, 0.0).astype(BF16)
    out = _dot_split_rhs(tril, jnp.concatenate(slabs, axis=0))
    return [out[i * CHUNK:(i + 1) * CHUNK] for i in range(len(slabs))]


def _chunks(tb):
    return [slice(c * CHUNK, (c + 1) * CHUNK) for c in range(tb // CHUNK)]


def _keep_tail(buf_ref):
    tb = buf_ref.shape[0] - 8
    buf_ref[0:8, :] = buf_ref[tb:tb + 8, :]


def _memkv_kernel(mem_ref, nw_ref, w_ref, o_ref):
    h = _rms(mem_ref[...], NORM_EPS) * nw_ref[...]
    o_ref[...] = _dot(h.astype(BF16), w_ref[...])


def _memkv(mem2d, nw, w_bf16):
    return pl.pallas_call(
        _memkv_kernel,
        out_shape=jax.ShapeDtypeStruct((MEM_LEN, 2 * XATTN_WIDTH), F32),
        name="memkv",
    )(mem2d, nw, w_bf16)


def _ssd_prep(pre_ref, zdt_ref, rows, cw_ref, cb_ref, dtb_ref, alog_ref):
    lo = 8 + rows.start
    conv = cb_ref[...] + pre_ref[lo:lo + CHUNK, :] * cw_ref[CONV_WIDTH - 1:CONV_WIDTH, :]
    for j in range(1, CONV_WIDTH):
        conv = conv + pre_ref[lo - j:lo - j + CHUNK, :] * cw_ref[CONV_WIDTH - 1 - j:CONV_WIDTH - j, :]
    xbc = _silu(conv)
    xs = xbc[:, :SSM_WIDTH]
    dt = _softplus(zdt_ref[rows, SSM_WIDTH:] + dtb_ref[...])
    return dict(xs=xs,
                bm=xbc[:, SSM_WIDTH:SSM_WIDTH + 256].astype(BF16),
                cm=xbc[:, SSM_WIDTH + 256:].astype(BF16),
                da=dt * (-jnp.exp(alog_ref[...])),
                xdt=xs * dt,
                zg=_silu(zdt_ref[rows, :SSM_WIDTH]))


_SSD_GROUPS = [slice(g * 256, (g + 1) * 256) for g in range(SSM_GROUPS)]
_SSD_GROUP_STATE = [slice(g * SSM_STATE, (g + 1) * SSM_STATE) for g in range(SSM_GROUPS)]


def _ssd_intra(p, acs_c):
    q_i = _iota((CHUNK, 128), 0)
    s_i = _iota((CHUNK, 128), 1) & (CHUNK - 1)
    eye2 = q_i == s_i
    causal2 = s_i <= q_i
    bdmask = (_iota((128, 128), 0) >> 6) == (_iota((128, 128), 1) >> 6)
    gl = _SSD_GROUP_STATE
    gps = [_dot_nt(p["cm"][:, gl[g]], jnp.concatenate([p["bm"][:, gl[g]]] * 2, axis=0))
           for g in range(SSM_GROUPS)]
    a_last = acs_c[CHUNK - 1:CHUNK, :]
    xend_c = (p["xdt"] * jnp.exp(a_last - acs_c)).astype(BF16)
    parts = []
    for j in range(SSM_HEADS // 2):
        lanes = slice(j * 128, (j + 1) * 128)
        colp = acs_c[:, lanes]
        rowp = jnp.sum(jnp.where(eye2, colp, 0.0), axis=0, keepdims=True)
        dm = jnp.exp(jnp.where(causal2, colp - rowp, -1e30))
        m = (gps[j // 2] * dm).astype(BF16)
        xd = p["xdt"][:, lanes]
        bd = jnp.where(bdmask, jnp.concatenate([xd, xd], axis=0), 0.0).astype(BF16)
        parts.append(_dot(m, bd))
    incs = [_dot_tn(p["bm"][:, gl[g]], xend_c[:, _SSD_GROUPS[g]]) for g in range(SSM_GROUPS)]
    return jnp.concatenate(parts, axis=1), incs, jnp.exp(a_last)


def _ssd_state_stage(prep, a_cs, intra, dsk_ref, nw_ref, state_ref, y_ref):
    groups, gl = _SSD_GROUPS, _SSD_GROUP_STATE
    st = [state_ref[:, groups[g]] for g in range(SSM_GROUPS)]
    for c, p in enumerate(prep):
        y_diag, incs, decay = intra[c]
        y_off = jnp.concatenate([_dot(p["cm"][:, gl[g]], st[g].astype(BF16)) for g in range(SSM_GROUPS)],
                                axis=1)
        y = (y_diag + y_off * jnp.exp(a_cs[c]) + dsk_ref[...] * p["xs"]) * p["zg"]
        y_ref[c * CHUNK:(c + 1) * CHUNK, :] = y
        st = [st[g] * decay[:, groups[g]] + incs[g] for g in range(SSM_GROUPS)]
    for g in range(SSM_GROUPS):
        state_ref[:, groups[g]] = st[g]

    y = y_ref[...]
    y = jnp.concatenate([_rms(y[:, groups[g]], NORM_EPS) for g in range(SSM_GROUPS)], axis=1)
    return (y * nw_ref[...]).astype(BF16)


def _head_block_mask():
    w = RWKV_WIDTH
    return (_iota((w, w), 0) >> 6) == (_iota((w, w), 1) >> 6)


def _rwkv_prep(u_ref, rows, mu_ref, w2a_ref, w0_ref, a0_ref, kk_ref, ka_ref):
    W = RWKV_WIDTH
    lo = 8 + rows.start
    u = u_ref[lo:lo + CHUNK, :]
    prev = u_ref[lo - 1:lo - 1 + CHUNK, :]
    us = u + (prev - u) * mu_ref[...]
    k = us[:, W:2 * W]
    lat = us[:, 4 * W:]
    lat = jnp.where(_iota(lat.shape, 1) < LORA, jnp.tanh(lat), lat)
    lora = _dot(lat.astype(BF16), w2a_ref[...])
    w_log = -_softplus(-(w0_ref[...] + lora[:, :W])) - 0.5
    a = _sigmoid(a0_ref[...] + lora[:, W:])
    head_ones = jnp.where(_head_block_mask(), 1.0, 0.0).astype(BF16)
    kkr = k * kk_ref[...]
    kk = kkr / jnp.maximum(jnp.sqrt(_dot((kkr * kkr).astype(BF16), head_ones)), L2_EPS)
    return dict(r=us[:, :W], v=us[:, 2 * W:3 * W], g=us[:, 3 * W:4 * W],
                logw=-jnp.exp(w_log),
                kk=kk, k2=k * (1.0 + (a - 1.0) * ka_ref[...]), b=kk * a)


def _rwkv_stages(prep, cum, rk_ref, lw_ref, lb_ref, state_ref, y_ref, run_pieces, side_work):
    W = RWKV_WIDTH
    blk = _head_block_mask()
    head_ones = jnp.where(blk, 1.0, 0.0).astype(BF16)
    head_avg = jnp.where(blk, 1.0 / HEAD_DIM, 0.0).astype(BF16)
    t_i = _iota((CHUNK, W), 0)
    j_i = _iota((CHUNK, W), 1) & (CHUNK - 1)
    strict = j_i < t_i
    incl = j_i <= t_i
    head_masks_rows = [(_iota((CHUNK, W), 1) >> 6) == h for h in range(RWKV_HEADS)]
    eye_pair = jnp.where((_iota((CHUNK, 128), 1) & (CHUNK - 1)) == _iota((CHUNK, 128), 0), 1.0, 0.0)
    first_of_pair = (_iota((CHUNK, W), 1) & 127) < CHUNK
    zero_blk = jnp.zeros((CHUNK, W), BF16)

    def bd(x):
        return jnp.where(blk, jnp.concatenate([x] * RWKV_HEADS, axis=0), jnp.zeros((W, W), x.dtype))

    ats, rts, vbs, a_ak, a_rk, a_rb, p_pairs, t_pairs = [], [], [], [], [], [], [], []
    for p, cum_c in zip(prep, cum):
        e_n = jnp.exp(-cum_c)
        at = (-p["kk"] * jnp.exp(cum_c - p["logw"])).astype(BF16)
        rt = p["r"] * jnp.exp(cum_c)
        kt = (p["k2"] * e_n).astype(BF16)
        bt = (p["b"] * e_n).astype(BF16)
        ats.append(at)
        rts.append(rt)
        vbs.append(p["v"].astype(BF16))
        x_cat = jnp.concatenate([at, rt.astype(BF16)], axis=0)
        y_stack = jnp.concatenate(
            [jnp.where(m, y, jnp.zeros_like(y)) for y in (kt, bt) for m in head_masks_rows], axis=0)
        aa = _dot_nt(x_cat, y_stack)
        a_ak.append(jnp.where(strict, aa[:CHUNK, :W], 0.0).astype(BF16))
        a_ab = jnp.where(strict, aa[:CHUNK, W:], 0.0)
        a_rk.append(jnp.where(incl, aa[CHUNK:, :W], 0.0).astype(BF16))
        a_rb.append(jnp.where(incl, aa[CHUNK:, W:], 0.0).astype(BF16))
        p_pairs.append([a_ab[:, :128].astype(BF16), a_ab[:, 128:].astype(BF16)])
        t_pairs.append([eye_pair, eye_pair])

    run_pieces(3, a_ab)
    for step in range(6):
        for c in range(len(p_pairs)):
            for h2 in range(RWKV_HEADS // 2):
                pb = p_pairs[c][h2]
                pt = jnp.concatenate([pb, t_pairs[c][h2].astype(BF16)], axis=1)
                rhs = jnp.concatenate([jnp.where(first_of_pair, pt, jnp.zeros_like(pt)),
                                       jnp.where(first_of_pair, jnp.zeros_like(pt), pt)], axis=0)
                res = _dot(pb, rhs)
                p_pairs[c][h2] = res[:, :128].astype(BF16)
                t_pairs[c][h2] = t_pairs[c][h2] + res[:, 128:]
        run_pieces(2, t_pairs[0][0])
        side_work(step)

    nc = len(prep)
    v_bds = [bd(vbs[c]) for c in range(nc)]
    avs = [_dot(a_ak[c], v_bds[c]).astype(BF16) for c in range(nc)]
    wus = [_dot(jnp.concatenate(t_pairs[c], axis=1).astype(BF16),
                jnp.concatenate([bd(ats[c]), bd(avs[c])], axis=1)).astype(BF16) for c in range(nc)]
    qs = [(rts[c] + _dot(a_rb[c], bd(wus[c][:, :W]))).astype(BF16) for c in range(nc)]
    y0s = [_dot(jnp.concatenate([a_rk[c], a_rb[c]], axis=1),
                jnp.concatenate([v_bds[c], bd(wus[c][:, W:])], axis=0)) for c in range(nc)]
    ps, ns, decays = [], [], []
    for c, (p, cum_c) in enumerate(zip(prep, cum)):
        cum_last = cum_c[CHUNK - 1:CHUNK, :]
        e_end = jnp.exp(cum_last - cum_c)
        decays.append(jnp.exp(cum_last))
        mn_lhs = jnp.concatenate([wus[c], jnp.concatenate([zero_blk, vbs[c]], axis=1)], axis=0)
        bk = jnp.concatenate([p["b"] * e_end, p["k2"] * e_end], axis=0).astype(BF16)
        mn = _dot_tn(mn_lhs, bk)
        ps.append(jnp.where(blk, mn[:W], 0.0).astype(BF16))
        ns.append(jnp.where(blk, mn[W:], 0.0))

    s_bd = state_ref[...]
    for c in range(len(prep)):
        sb = s_bd.astype(BF16)
        y_ref[c * CHUNK:(c + 1) * CHUNK, :] = _dot_nt(qs[c], sb) + y0s[c]
        s_bd = s_bd * decays[c] + _dot(sb, ps[c]) + ns[c]
    state_ref[...] = s_bd

    whole = lambda key: jnp.concatenate([p[key] for p in prep], axis=0)
    r, k2, v, g = whole("r"), whole("k2"), whole("v"), whole("g")
    y = y_ref[...]
    yc = y - _dot(y.astype(BF16), head_avg)
    var = _dot((yc * yc).astype(BF16), head_avg)
    yn = yc * lax.rsqrt(var + LNX_EPS) * lw_ref[...] + lb_ref[...]
    bonus = _dot((r * k2 * rk_ref[...]).astype(BF16), head_ones) * v
    return ((yn + bonus) * _silu(g)).astype(BF16)


def _xattn_block(u, kt_ref, vb_ref, ones_ref):
    s = _dot(u[:, :XATTN_WIDTH].astype(BF16), kt_ref[...])
    ps = []
    for h in range(XATTN_HEADS):
        sh = s[:, h * MEM_LEN:(h + 1) * MEM_LEN]
        ps.append(jnp.exp(sh - jnp.max(sh, axis=-1, keepdims=True)))
    p = jnp.concatenate(ps, axis=1).astype(BF16)
    o = _dot(p, vb_ref[...]) / _dot(p, ones_ref[...])
    return (o * _silu(u[:, XATTN_WIDTH:])).astype(BF16)


def _inproj_pieces(hb, rows, wmain_ref, wdt_ref, wrest_ref, rw_u, ssd_pre, zdt, xa):
    lhs_rows = hb[rows]
    pieces = []
    for dst, row0, dcol, w_ref, wcol, width in (
            (rw_u, 8, 0, wrest_ref, 0, RWKV_IN),
            (ssd_pre, 8, 0, wmain_ref, 0, CONV_DIM),
            (zdt, 0, 0, wmain_ref, CONV_DIM, SSM_WIDTH),
            (zdt, 0, SSM_WIDTH, wdt_ref, 0, SSM_WIDTH),
            (xa, 0, 0, wrest_ref, RWKV_IN, XATTN_IN)):
        for c in range(0, width, PIECE):
            w = min(PIECE, width - c)

            def piece(anchor=None, dst=dst, row0=row0, d0=dcol + c, w_ref=w_ref, w0=wcol + c, w=w):
                lhs = lhs_rows if anchor is None else _after(lhs_rows, anchor)
                dst[row0 + rows.start:row0 + rows.stop, d0:d0 + w] = _dot(lhs, w_ref[:, w0:w0 + w])
            pieces.append(piece)
    return pieces


def _after(x_bf16, anchor):
    bits = pltpu.bitcast(anchor[:16, :128], jnp.int32)
    zero = lax.shift_right_logical(lax.shift_right_logical(bits, 16), 16)
    zero = jnp.concatenate([zero.astype(F32).astype(BF16)] * (x_bf16.shape[1] // 128), axis=1)
    return jnp.concatenate([x_bf16[:16] + zero, x_bf16[16:]], axis=0)


def _layer_kernel(x_ref, xnext_ref, prew_ref, wmain_ref, wdt_ref, wrest_ref,
                  cw_ref, cb_ref, dtb_ref, alog_ref, dsk_ref, snw_ref,
                  mu_ref, w2a_ref, w0_ref, a0_ref, kk_ref, ka_ref, rk_ref, lw_ref, lb_ref,
                  kt_ref, vb_ref, ones_ref, wout_ref, postw_ref,
                  o_ref,
                  rw_u, ssd_pre, zdt, xa, ssd_state, ssd_y, rw_state, rw_y):
    step = pl.program_id(0)
    tb = x_ref.shape[0]
    bufs = (rw_u, ssd_pre, zdt, xa)
    weights = (wmain_ref, wdt_ref, wrest_ref)
    normed = lambda x: (_rms(x, NORM_EPS) * prew_ref[...]).astype(BF16)

    @pl.when(step == 0)
    def _():
        rw_u[0:8, :] = jnp.zeros((8, RWKV_IN), F32)
        ssd_pre[0:8, :] = jnp.zeros((8, CONV_DIM), F32)
        ssd_state[...] = jnp.zeros_like(ssd_state)
        rw_state[...] = jnp.zeros_like(rw_state)
        for piece in _inproj_pieces(normed(x_ref[...]), slice(0, tb), *weights, *bufs):
            piece()

    hb = normed(xnext_ref[...])
    chunks = _chunks(tb)
    half = len(chunks) // 2
    first = _inproj_pieces(hb, slice(0, half * CHUNK), *weights, *bufs)
    second = _inproj_pieces(hb, slice(half * CHUNK, tb), *weights, *bufs)
    y_mem = _xattn_block(xa[...], kt_ref, vb_ref, ones_ref)

    def run(pieces, count, anchor):
        for _ in range(min(count, len(pieces))):
            pieces.pop(0)(anchor)

    rw_prep, sd_prep, cum, a_cs = [], [], [], []
    per_slab = -(-len(first) // (len(chunks) - half))
    for c, rows in enumerate(chunks):
        rw_prep.append(_rwkv_prep(rw_u, rows, mu_ref, w2a_ref, w0_ref, a0_ref, kk_ref, ka_ref))
        sd_prep.append(_ssd_prep(ssd_pre, zdt, rows, cw_ref, cb_ref, dtb_ref, alog_ref))
        if c >= half:
            run(first, per_slab, rw_prep[-1]["kk"])
        done = len(rw_prep)
        if (done * CHUNK) % CUMSUM_BLOCK == 0 or done * CHUNK == tb:
            cum += _chunk_cumsum([p["logw"] for p in rw_prep[len(cum):]])
            a_cs += _chunk_cumsum([p["da"] for p in sd_prep[len(a_cs):]])
    _keep_tail(rw_u)
    _keep_tail(ssd_pre)
    assert not first
    intra = []

    def ssd_side_work(step_, steps=6):
        upto = -(-(step_ + 1) * len(sd_prep) // steps)
        while len(intra) < upto:
            intra.append(_ssd_intra(sd_prep[len(intra)], a_cs[len(intra)]))

    y_rw = _rwkv_stages(rw_prep, cum, rk_ref, lw_ref, lb_ref, rw_state, rw_y,
                        functools.partial(run, second), ssd_side_work)
    assert not second and len(intra) == len(sd_prep)
    y_ssm = _ssd_state_stage(sd_prep, a_cs, intra, dsk_ref, snw_ref, ssd_state, ssd_y)
    d = _dot(jnp.concatenate([y_ssm, y_rw, y_mem], axis=1), wout_ref[...])
    o_ref[...] = x_ref[...] + _rms(d, NORM_EPS) * postw_ref[...]


def _layer(x2d, params):
    t = x2d.shape[0]
    tb = min(TB, t)
    assert t % tb == 0 and tb % CHUNK == 0
    n = t // tb
    rows = lambda i: (i, 0)
    next_rows = lambda i: (jnp.minimum(i + 1, n - 1), 0)
    whole = lambda a: pl.BlockSpec(a.shape, lambda i: (0,) * a.ndim)
    return pl.pallas_call(
        _layer_kernel,
        grid=(n,),
        in_specs=[pl.BlockSpec((tb, D_MODEL), rows), pl.BlockSpec((tb, D_MODEL), next_rows)]
        + [whole(p) for p in params],
        out_specs=pl.BlockSpec((tb, D_MODEL), rows),
        out_shape=jax.ShapeDtypeStruct((t, D_MODEL), F32),
        scratch_shapes=[pltpu.VMEM((tb + 8, RWKV_IN), F32),
                        pltpu.VMEM((tb + 8, CONV_DIM), F32),
                        pltpu.VMEM((tb, 2 * SSM_WIDTH), F32),
                        pltpu.VMEM((tb, XATTN_IN), F32),
                        pltpu.VMEM((SSM_STATE, SSM_WIDTH), F32),
                        pltpu.VMEM((tb, SSM_WIDTH), F32),
                        pltpu.VMEM((RWKV_WIDTH, RWKV_WIDTH), F32),
                        pltpu.VMEM((tb, RWKV_WIDTH), F32)],
        compiler_params=pltpu.CompilerParams(dimension_semantics=("arbitrary",),
                                             vmem_limit_bytes=VMEM_LIMIT),
        name="layer",
    )(x2d, x2d, *params)


def _block_diag_heads(blocks):
    h, r, c = blocks.shape
    eye = jnp.eye(h, dtype=blocks.dtype)
    return (eye[:, None, :, None] * blocks[:, :, None, :]).reshape(h * r, h * c)


def kernel(x, mem, mem_norm_w, w_mem_kv, pre_norm_w, w_in, conv_w, conv_b, dt_bias, a_log, d_skip,
           ssm_norm_w, shift_mu, w0, w2, a0, a2, k_k, k_a, r_k, lnx_w, lnx_b, w_out, post_norm_w):
    assert x.shape[0] == 1 and mem.shape[0] == 1
    rep = lambda p: jnp.repeat(p, HEAD_DIM, axis=-1)
    row = lambda p: p.reshape(1, -1)

    kv = _memkv(mem[0], row(mem_norm_w), w_mem_kv.astype(BF16))
    mk = kv[:, :XATTN_WIDTH].reshape(MEM_LEN, XATTN_HEADS, HEAD_DIM)
    mv = kv[:, XATTN_WIDTH:].reshape(MEM_LEN, XATTN_HEADS, HEAD_DIM)
    kt_bd = _block_diag_heads(jnp.transpose(mk, (1, 2, 0)) * (HEAD_DIM ** -0.5)).astype(BF16)
    v_bd = _block_diag_heads(jnp.transpose(mv, (1, 0, 2))).astype(BF16)
    ones_bd = _block_diag_heads(jnp.ones((XATTN_HEADS, MEM_LEN, HEAD_DIM), BF16))

    w_main = w_in[:, :, :CONV_DIM + SSM_WIDTH].astype(BF16)
    w_dt = jnp.repeat(w_in[:, :, CONV_DIM + SSM_WIDTH:SSM_IN].astype(BF16), HEAD_DIM, axis=-1)
    w_rest = w_in[:, :, SSM_IN:].astype(BF16)
    w_out_k = w_out.astype(BF16)
    zeros = jnp.zeros((DEPTH, LORA, RWKV_WIDTH), F32)
    w2a = jnp.concatenate([jnp.concatenate([w2, zeros], axis=-1),
                           jnp.concatenate([zeros, a2], axis=-1)], axis=1).astype(BF16)

    xc = x[0]
    for i in range(DEPTH):
        params = (row(pre_norm_w[i]), w_main[i], w_dt[i], w_rest[i],
                  conv_w[i], row(conv_b[i]), row(rep(dt_bias[i])), row(rep(a_log[i])),
                  row(rep(d_skip[i])), row(ssm_norm_w[i]),
                  row(shift_mu[i]), w2a[i], row(w0[i]), row(a0[i]), row(k_k[i]), row(k_a[i]),
                  row(r_k[i]), row(lnx_w[i]), row(lnx_b[i]),
                  kt_bd, v_bd, ones_bd, w_out_k[i], row(post_norm_w[i]))
        xc = _layer(xc, params)
    return xc[None]
```

```python
import functools

import jax
import jax.numpy as jnp
from jax import lax
from jax.experimental import pallas as pl
from jax.experimental.pallas import tpu as pltpu

F32 = jnp.float32
BF16 = jnp.bfloat16

D_MODEL = 1024
DEPTH = 4
HEAD_DIM = 64
CHUNK = 64
MEM_LEN = 256
SSM_WIDTH = 512
SSM_HEADS = 8
SSM_GROUPS = 2
SSM_STATE = 128
CONV_WIDTH = 4
CONV_DIM = SSM_WIDTH + 2 * SSM_GROUPS * SSM_STATE
SSM_IN = CONV_DIM + SSM_WIDTH + SSM_HEADS
RWKV_WIDTH = 256
RWKV_HEADS = 4
LORA = 64
RWKV_IN = 4 * RWKV_WIDTH + 2 * LORA
XATTN_WIDTH = 256
XATTN_HEADS = 4
XATTN_IN = 2 * XATTN_WIDTH
NORM_EPS = 1e-6
LNX_EPS = 64e-5
L2_EPS = 1e-12

TB = 512
CUMSUM_BLOCK = 256
VMEM_LIMIT = 56 * 1024 * 1024


_dot = functools.partial(jnp.dot, preferred_element_type=F32)
_dot_nt = functools.partial(lax.dot_general, dimension_numbers=(((1,), (1,)), ((), ())),
                            preferred_element_type=F32)
_dot_tn = functools.partial(lax.dot_general, dimension_numbers=(((0,), (0,)), ((), ())),
                            preferred_element_type=F32)


def _dot_split_rhs(lhs_bf16, x):
    hi = x.astype(BF16)
    lo = (x - hi.astype(F32)).astype(BF16)
    return _dot(lhs_bf16, lo) + _dot(lhs_bf16, hi)


def _sigmoid(x):
    return 1.0 / (1.0 + jnp.exp(-x))


def _silu(x):
    return x * _sigmoid(x)


def _softplus(x):
    return jnp.maximum(x, 0.0) + jnp.log(1.0 + jnp.exp(-jnp.abs(x)))


def _rms(x, eps):
    return x * lax.rsqrt(jnp.mean(x * x, axis=-1, keepdims=True) + eps)


def _iota(shape, dim):
    return lax.broadcasted_iota(jnp.int32, shape, dim)


def _constants():
    n = CUMSUM_BLOCK
    assert n == RWKV_WIDTH and CHUNK == HEAD_DIM
    r, c = _iota((n, n), 0), _iota((n, n), 1)
    same_chunk = (r >> 6) == (c >> 6)
    q_i = _iota((CHUNK, 128), 0)
    s_i = _iota((CHUNK, 128), 1) & (CHUNK - 1)
    return dict(
        tril=jnp.where(same_chunk & (c <= r), 1.0, 0.0).astype(BF16),
        head_blk=same_chunk,
        head_ones=jnp.where(same_chunk, 1.0, 0.0).astype(BF16),
        head_avg=jnp.where(same_chunk, 1.0 / HEAD_DIM, 0.0).astype(BF16),
        eye2=q_i == s_i, causal2=s_i <= q_i,
        pair_blk=(_iota((128, 128), 0) >> 6) == (_iota((128, 128), 1) >> 6))


def _chunk_cumsum(slabs, tril):
    n = CHUNK * len(slabs)
    out = _dot_split_rhs(tril[:n, :n], jnp.concatenate(slabs, axis=0))
    return [out[i * CHUNK:(i + 1) * CHUNK] for i in range(len(slabs))]


def _chunks(tb):
    return [slice(c * CHUNK, (c + 1) * CHUNK) for c in range(tb // CHUNK)]


def _keep_tail(buf_ref):
    tb = buf_ref.shape[0] - 8
    buf_ref[0:8, :] = buf_ref[tb:tb + 8, :]


def _memkv_kernel(mem_ref, nw_ref, w_ref, o_ref):
    h = _rms(mem_ref[...], NORM_EPS) * nw_ref[...]
    o_ref[...] = _dot(h.astype(BF16), w_ref[...])


def _memkv(mem2d, nw, w_bf16):
    return pl.pallas_call(
        _memkv_kernel,
        out_shape=jax.ShapeDtypeStruct((MEM_LEN, 2 * XATTN_WIDTH), F32),
        name="memkv",
    )(mem2d, nw, w_bf16)


def _ssd_prep(pre_ref, zdt_ref, rows, cw_ref, cb_ref, dtb_ref, alog_ref):
    lo = 8 + rows.start
    conv = cb_ref[...] + pre_ref[lo:lo + CHUNK, :] * cw_ref[CONV_WIDTH - 1:CONV_WIDTH, :]
    for j in range(1, CONV_WIDTH):
        conv = conv + pre_ref[lo - j:lo - j + CHUNK, :] * cw_ref[CONV_WIDTH - 1 - j:CONV_WIDTH - j, :]
    xbc = _silu(conv)
    xs = xbc[:, :SSM_WIDTH]
    dt = _softplus(zdt_ref[rows, SSM_WIDTH:] + dtb_ref[...])
    return dict(xs=xs,
                bm=xbc[:, SSM_WIDTH:SSM_WIDTH + 256].astype(BF16),
                cm=xbc[:, SSM_WIDTH + 256:].astype(BF16),
                da=dt * (-jnp.exp(alog_ref[...])),
                xdt=xs * dt,
                zg=_silu(zdt_ref[rows, :SSM_WIDTH]))


_SSD_GROUPS = [slice(g * 256, (g + 1) * 256) for g in range(SSM_GROUPS)]
_SSD_GROUP_STATE = [slice(g * SSM_STATE, (g + 1) * SSM_STATE) for g in range(SSM_GROUPS)]


def _ssd_intra(p, acs_c, consts):
    eye2, causal2, bdmask = consts["eye2"], consts["causal2"], consts["pair_blk"]
    gl = _SSD_GROUP_STATE
    gps = [_dot_nt(p["cm"][:, gl[g]], jnp.concatenate([p["bm"][:, gl[g]]] * 2, axis=0))
           for g in range(SSM_GROUPS)]
    a_last = acs_c[CHUNK - 1:CHUNK, :]
    xend_c = (p["xdt"] * jnp.exp(a_last - acs_c)).astype(BF16)
    parts = []
    for j in range(SSM_HEADS // 2):
        lanes = slice(j * 128, (j + 1) * 128)
        colp = acs_c[:, lanes]
        rowp = jnp.sum(jnp.where(eye2, colp, 0.0), axis=0, keepdims=True)
        dm = jnp.exp(jnp.where(causal2, colp - rowp, -1e30))
        m = (gps[j // 2] * dm).astype(BF16)
        xd = p["xdt"][:, lanes]
        bd = jnp.where(bdmask, jnp.concatenate([xd, xd], axis=0), 0.0).astype(BF16)
        parts.append(_dot(m, bd))
    incs = [_dot_tn(p["bm"][:, gl[g]], xend_c[:, _SSD_GROUPS[g]]) for g in range(SSM_GROUPS)]
    return jnp.concatenate(parts, axis=1), incs, jnp.exp(a_last)


def _ssd_state_stage(prep, a_cs, intra, dsk_ref, nw_ref, state_ref, y_ref):
    groups, gl = _SSD_GROUPS, _SSD_GROUP_STATE
    st = [state_ref[:, groups[g]] for g in range(SSM_GROUPS)]
    for c, p in enumerate(prep):
        y_diag, incs, decay = intra[c]
        y_off = jnp.concatenate([_dot(p["cm"][:, gl[g]], st[g].astype(BF16)) for g in range(SSM_GROUPS)],
                                axis=1)
        y = (y_diag + y_off * jnp.exp(a_cs[c]) + dsk_ref[...] * p["xs"]) * p["zg"]
        y_ref[c * CHUNK:(c + 1) * CHUNK, :] = y
        st = [st[g] * decay[:, groups[g]] + incs[g] for g in range(SSM_GROUPS)]
    for g in range(SSM_GROUPS):
        state_ref[:, groups[g]] = st[g]

    y = y_ref[...]
    y = jnp.concatenate([_rms(y[:, groups[g]], NORM_EPS) for g in range(SSM_GROUPS)], axis=1)
    return (y * nw_ref[...]).astype(BF16)


def _rwkv_prep(u_ref, rows, mu_ref, w2a_ref, w0_ref, a0_ref, kk_ref, ka_ref, consts):
    W = RWKV_WIDTH
    lo = 8 + rows.start
    u = u_ref[lo:lo + CHUNK, :]
    prev = u_ref[lo - 1:lo - 1 + CHUNK, :]
    us = u + (prev - u) * mu_ref[...]
    k = us[:, W:2 * W]
    lat = us[:, 4 * W:]
    lat = jnp.where(_iota(lat.shape, 1) < LORA, jnp.tanh(lat), lat)
    lora = _dot(lat.astype(BF16), w2a_ref[...])
    w_log = -_softplus(-(w0_ref[...] + lora[:, :W])) - 0.5
    a = _sigmoid(a0_ref[...] + lora[:, W:])
    kkr = k * kk_ref[...]
    kk = kkr / jnp.maximum(jnp.sqrt(_dot((kkr * kkr).astype(BF16), consts["head_ones"])), L2_EPS)
    return dict(r=us[:, :W], v=us[:, 2 * W:3 * W], g=us[:, 3 * W:4 * W],
                logw=-jnp.exp(w_log),
                kk=kk, k2=k * (1.0 + (a - 1.0) * ka_ref[...]), b=kk * a)


def _rwkv_stages(prep, cum, rk_ref, lw_ref, lb_ref, state_ref, y_ref, consts, side_work):
    W = RWKV_WIDTH
    blk, head_ones, head_avg = consts["head_blk"], consts["head_ones"], consts["head_avg"]
    t_i = _iota((CHUNK, W), 0)
    j_i = _iota((CHUNK, W), 1) & (CHUNK - 1)
    strict = j_i < t_i
    incl = j_i <= t_i
    head_masks_rows = [(_iota((CHUNK, W), 1) >> 6) == h for h in range(RWKV_HEADS)]
    eye_pair = jnp.where((_iota((CHUNK, 128), 1) & (CHUNK - 1)) == _iota((CHUNK, 128), 0), 1.0, 0.0)
    first_of_pair = jnp.where((_iota((CHUNK, W), 1) & 127) < CHUNK, 1.0, 0.0).astype(BF16)
    second_of_pair = 1.0 - first_of_pair
    zero_blk = jnp.zeros((CHUNK, W), BF16)

    def bd(x):
        return jnp.concatenate([x] * RWKV_HEADS, axis=0) * head_ones

    ats, rts, vbs, a_ak, a_rk, a_rb, p_pairs, t_pairs = [], [], [], [], [], [], [], []
    for p, cum_c in zip(prep, cum):
        e_n = jnp.exp(-cum_c)
        at = (-p["kk"] * jnp.exp(cum_c - p["logw"])).astype(BF16)
        rt = p["r"] * jnp.exp(cum_c)
        kt = (p["k2"] * e_n).astype(BF16)
        bt = (p["b"] * e_n).astype(BF16)
        ats.append(at)
        rts.append(rt)
        vbs.append(p["v"].astype(BF16))
        x_cat = jnp.concatenate([at, rt.astype(BF16)], axis=0)
        y_stack = jnp.concatenate(
            [jnp.where(m, y, jnp.zeros_like(y)) for y in (kt, bt) for m in head_masks_rows], axis=0)
        aa = _dot_nt(x_cat, y_stack)
        a_ak.append(jnp.where(strict, aa[:CHUNK, :W], 0.0).astype(BF16))
        a_ab = jnp.where(strict, aa[:CHUNK, W:], 0.0)
        a_rk.append(jnp.where(incl, aa[CHUNK:, :W], 0.0).astype(BF16))
        a_rb.append(jnp.where(incl, aa[CHUNK:, W:], 0.0).astype(BF16))
        p_pairs.append([a_ab[:, :128].astype(BF16), a_ab[:, 128:].astype(BF16)])
        t_pairs.append([eye_pair, eye_pair])

    for step in range(6):
        for c in range(len(p_pairs)):
            for h2 in range(RWKV_HEADS // 2):
                pb = p_pairs[c][h2]
                pt = jnp.concatenate([pb, t_pairs[c][h2].astype(BF16)], axis=1)
                rhs = jnp.concatenate([pt * first_of_pair, pt * second_of_pair], axis=0)
                res = _dot(pb, rhs)
                p_pairs[c][h2] = res[:, :128].astype(BF16)
                t_pairs[c][h2] = t_pairs[c][h2] + res[:, 128:]
        side_work(step)

    nc = len(prep)
    v_bds = [bd(vbs[c]) for c in range(nc)]
    avs = [_dot(a_ak[c], v_bds[c]).astype(BF16) for c in range(nc)]
    wus = [_dot(jnp.concatenate(t_pairs[c], axis=1).astype(BF16),
                jnp.concatenate([bd(ats[c]), bd(avs[c])], axis=1)).astype(BF16) for c in range(nc)]
    qs = [(rts[c] + _dot(a_rb[c], bd(wus[c][:, :W]))).astype(BF16) for c in range(nc)]
    y0s = [_dot(jnp.concatenate([a_rk[c], a_rb[c]], axis=1),
                jnp.concatenate([v_bds[c], bd(wus[c][:, W:])], axis=0)) for c in range(nc)]
    ps, ns, decays = [], [], []
    for c, (p, cum_c) in enumerate(zip(prep, cum)):
        cum_last = cum_c[CHUNK - 1:CHUNK, :]
        e_end = jnp.exp(cum_last - cum_c)
        decays.append(jnp.exp(cum_last))
        mn_lhs = jnp.concatenate([wus[c], jnp.concatenate([zero_blk, vbs[c]], axis=1)], axis=0)
        bk = jnp.concatenate([p["b"] * e_end, p["k2"] * e_end], axis=0).astype(BF16)
        mn = _dot_tn(mn_lhs, bk)
        ps.append(jnp.where(blk, mn[:W], 0.0).astype(BF16))
        ns.append(jnp.where(blk, mn[W:], 0.0))

    s_bd = state_ref[...]
    for c in range(len(prep)):
        sb = s_bd.astype(BF16)
        y_ref[c * CHUNK:(c + 1) * CHUNK, :] = _dot_nt(qs[c], sb) + y0s[c]
        s_bd = s_bd * decays[c] + _dot(sb, ps[c]) + ns[c]
    state_ref[...] = s_bd

    whole = lambda key: jnp.concatenate([p[key] for p in prep], axis=0)
    r, k2, v, g = whole("r"), whole("k2"), whole("v"), whole("g")
    y = y_ref[...]
    yc = y - _dot(y.astype(BF16), head_avg)
    var = _dot((yc * yc).astype(BF16), head_avg)
    yn = yc * lax.rsqrt(var + LNX_EPS) * lw_ref[...] + lb_ref[...]
    bonus = _dot((r * k2 * rk_ref[...]).astype(BF16), head_ones) * v
    return ((yn + bonus) * _silu(g)).astype(BF16)


def _xattn_block(u, kt_ref, vb_ref, ones_ref):
    s = _dot(u[:, :XATTN_WIDTH].astype(BF16), kt_ref[...])
    ps = []
    for h in range(XATTN_HEADS):
        sh = s[:, h * MEM_LEN:(h + 1) * MEM_LEN]
        ps.append(jnp.exp(sh - jnp.max(sh, axis=-1, keepdims=True)))
    p = jnp.concatenate(ps, axis=1).astype(BF16)
    o = _dot(p, vb_ref[...]) / _dot(p, ones_ref[...])
    return (o * _silu(u[:, XATTN_WIDTH:])).astype(BF16)


def _inproj(x, prew_ref, wmain_ref, wdt_ref, wrest_ref, rw_u, ssd_pre, zdt, xa):
    tb = x.shape[0]
    hb = (_rms(x, NORM_EPS) * prew_ref[...]).astype(BF16)
    rw_u[8:8 + tb, :] = _dot(hb, wrest_ref[:, :RWKV_IN])
    ssd_pre[8:8 + tb, :] = _dot(hb, wmain_ref[:, :CONV_DIM])
    zdt[:, :SSM_WIDTH] = _dot(hb, wmain_ref[:, CONV_DIM:])
    zdt[:, SSM_WIDTH:] = _dot(hb, wdt_ref[...])
    xa[...] = _dot(hb, wrest_ref[:, RWKV_IN:])


def _layer_kernel(x_ref, prew_ref, wmain_ref, wdt_ref, wrest_ref,
                  cw_ref, cb_ref, dtb_ref, alog_ref, dsk_ref, snw_ref,
                  mu_ref, w2a_ref, w0_ref, a0_ref, kk_ref, ka_ref, rk_ref, lw_ref, lb_ref,
                  kt_ref, vb_ref, ones_ref, wout_ref, postw_ref,
                  o_ref,
                  rw_u, ssd_pre, zdt, xa, ssd_state, ssd_y, rw_state, rw_y):
    tb = x_ref.shape[0]

    @pl.when(pl.program_id(0) == 0)
    def _():
        rw_u[0:8, :] = jnp.zeros((8, RWKV_IN), F32)
        ssd_pre[0:8, :] = jnp.zeros((8, CONV_DIM), F32)
        ssd_state[...] = jnp.zeros_like(ssd_state)
        rw_state[...] = jnp.zeros_like(rw_state)

    _inproj(x_ref[...], prew_ref, wmain_ref, wdt_ref, wrest_ref, rw_u, ssd_pre, zdt, xa)
    y_mem = _xattn_block(xa[...], kt_ref, vb_ref, ones_ref)

    consts = _constants()
    rw_prep, sd_prep, cum, a_cs = [], [], [], []
    for rows in _chunks(tb):
        rw_prep.append(_rwkv_prep(rw_u, rows, mu_ref, w2a_ref, w0_ref, a0_ref, kk_ref, ka_ref, consts))
        sd_prep.append(_ssd_prep(ssd_pre, zdt, rows, cw_ref, cb_ref, dtb_ref, alog_ref))
        done = len(rw_prep)
        if (done * CHUNK) % CUMSUM_BLOCK == 0 or done * CHUNK == tb:
            cum += _chunk_cumsum([p["logw"] for p in rw_prep[len(cum):]], consts["tril"])
            a_cs += _chunk_cumsum([p["da"] for p in sd_prep[len(a_cs):]], consts["tril"])
    _keep_tail(rw_u)
    _keep_tail(ssd_pre)
    intra = []

    def ssd_side_work(step, steps=6):
        upto = -(-(step + 1) * len(sd_prep) // steps)
        while len(intra) < upto:
            intra.append(_ssd_intra(sd_prep[len(intra)], a_cs[len(intra)], consts))

    y_rw = _rwkv_stages(rw_prep, cum, rk_ref, lw_ref, lb_ref, rw_state, rw_y, consts, ssd_side_work)
    assert len(intra) == len(sd_prep)
    y_ssm = _ssd_state_stage(sd_prep, a_cs, intra, dsk_ref, snw_ref, ssd_state, ssd_y)
    d = _dot(jnp.concatenate([y_ssm, y_rw, y_mem], axis=1), wout_ref[...])
    o_ref[...] = x_ref[...] + _rms(d, NORM_EPS) * postw_ref[...]


def _layer(x2d, params):
    t = x2d.shape[0]
    tb = min(TB, t)
    assert t % tb == 0 and tb % CHUNK == 0
    rows = lambda i: (i, 0)
    whole = lambda a: pl.BlockSpec(a.shape, lambda i: (0,) * a.ndim)
    return pl.pallas_call(
        _layer_kernel,
        grid=(t // tb,),
        in_specs=[pl.BlockSpec((tb, D_MODEL), rows)] + [whole(p) for p in params],
        out_specs=pl.BlockSpec((tb, D_MODEL), rows),
        out_shape=jax.ShapeDtypeStruct((t, D_MODEL), F32),
        scratch_shapes=[pltpu.VMEM((tb + 8, RWKV_IN), F32),
                        pltpu.VMEM((tb + 8, CONV_DIM), F32),
                        pltpu.VMEM((tb, 2 * SSM_WIDTH), F32),
                        pltpu.VMEM((tb, XATTN_IN), F32),
                        pltpu.VMEM((SSM_STATE, SSM_WIDTH), F32),
                        pltpu.VMEM((tb, SSM_WIDTH), F32),
                        pltpu.VMEM((RWKV_WIDTH, RWKV_WIDTH), F32),
                        pltpu.VMEM((tb, RWKV_WIDTH), F32)],
        compiler_params=pltpu.CompilerParams(dimension_semantics=("arbitrary",),
                                             vmem_limit_bytes=VMEM_LIMIT),
        name="layer",
    )(x2d, *params)


def _block_diag_heads(blocks):
    h, r, c = blocks.shape
    eye = jnp.eye(h, dtype=blocks.dtype)
    return (eye[:, None, :, None] * blocks[:, :, None, :]).reshape(h * r, h * c)


def kernel(x, mem, mem_norm_w, w_mem_kv, pre_norm_w, w_in, conv_w, conv_b, dt_bias, a_log, d_skip,
           ssm_norm_w, shift_mu, w0, w2, a0, a2, k_k, k_a, r_k, lnx_w, lnx_b, w_out, post_norm_w):
    assert x.shape[0] == 1 and mem.shape[0] == 1
    rep = lambda p: jnp.repeat(p, HEAD_DIM, axis=-1)
    row = lambda p: p.reshape(1, -1)

    kv = _memkv(mem[0], row(mem_norm_w), w_mem_kv.astype(BF16))
    mk = kv[:, :XATTN_WIDTH].reshape(MEM_LEN, XATTN_HEADS, HEAD_DIM)
    mv = kv[:, XATTN_WIDTH:].reshape(MEM_LEN, XATTN_HEADS, HEAD_DIM)
    kt_bd = _block_diag_heads(jnp.transpose(mk, (1, 2, 0)) * (HEAD_DIM ** -0.5)).astype(BF16)
    v_bd = _block_diag_heads(jnp.transpose(mv, (1, 0, 2))).astype(BF16)
    ones_bd = _block_diag_heads(jnp.ones((XATTN_HEADS, MEM_LEN, HEAD_DIM), BF16))

    w_main = w_in[:, :, :CONV_DIM + SSM_WIDTH].astype(BF16)
    w_dt = jnp.repeat(w_in[:, :, CONV_DIM + SSM_WIDTH:SSM_IN].astype(BF16), HEAD_DIM, axis=-1)
    w_rest = w_in[:, :, SSM_IN:].astype(BF16)
    w_out_k = w_out.astype(BF16)
    zeros = jnp.zeros((DEPTH, LORA, RWKV_WIDTH), F32)
    w2a = jnp.concatenate([jnp.concatenate([w2, zeros], axis=-1),
                           jnp.concatenate([zeros, a2], axis=-1)], axis=1).astype(BF16)

    xc = x[0]
    for i in range(DEPTH):
        params = (row(pre_norm_w[i]), w_main[i], w_dt[i], w_rest[i],
                  conv_w[i], row(conv_b[i]), row(rep(dt_bias[i])), row(rep(a_log[i])),
                  row(rep(d_skip[i])), row(ssm_norm_w[i]),
                  row(shift_mu[i]), w2a[i], row(w0[i]), row(a0[i]), row(k_k[i]), row(k_a[i]),
                  row(r_k[i]), row(lnx_w[i]), row(lnx_b[i]),
                  kt_bd, v_bd, ones_bd, w_out_k[i], row(post_norm_w[i]))
        xc = _layer(xc, params)
    return xc[None]
```

```python
import functools

import jax
import jax.numpy as jnp
from jax import lax
from jax.experimental import pallas as pl
from jax.experimental.pallas import tpu as pltpu

F32 = jnp.float32
BF16 = jnp.bfloat16

D_MODEL = 1024
DEPTH = 4
HEAD_DIM = 64
CHUNK = 64
MEM_LEN = 256
SSM_WIDTH = 512
SSM_HEADS = 8
SSM_GROUPS = 2
SSM_STATE = 128
CONV_WIDTH = 4
CONV_DIM = SSM_WIDTH + 2 * SSM_GROUPS * SSM_STATE
SSM_IN = CONV_DIM + SSM_WIDTH + SSM_HEADS
RWKV_WIDTH = 256
RWKV_HEADS = 4
LORA = 64
RWKV_IN = 4 * RWKV_WIDTH + 2 * LORA
XATTN_WIDTH = 256
XATTN_HEADS = 4
XATTN_IN = 2 * XATTN_WIDTH
NORM_EPS = 1e-6
LNX_EPS = 64e-5
L2_EPS = 1e-12

TB = 512
CUMSUM_BLOCK = 256
VMEM_LIMIT = 56 * 1024 * 1024


_dot = functools.partial(jnp.dot, preferred_element_type=F32)
_dot_nt = functools.partial(lax.dot_general, dimension_numbers=(((1,), (1,)), ((), ())),
                            preferred_element_type=F32)
_dot_tn = functools.partial(lax.dot_general, dimension_numbers=(((0,), (0,)), ((), ())),
                            preferred_element_type=F32)


def _dot_split_rhs(lhs_bf16, x):
    hi = x.astype(BF16)
    lo = (x - hi.astype(F32)).astype(BF16)
    return _dot(lhs_bf16, lo) + _dot(lhs_bf16, hi)


def _dot_split_lhs(x, rhs_bf16):
    hi = x.astype(BF16)
    lo = (x - hi.astype(F32)).astype(BF16)
    return _dot(lo, rhs_bf16) + _dot(hi, rhs_bf16)


def _sigmoid(x):
    return 1.0 / (1.0 + jnp.exp(-x))


def _silu(x):
    return x * _sigmoid(x)


LOG2E = 1.4426950408889634
LN2 = 0.6931471805599453


def _softplus(x):
    t = x * LOG2E
    return (jnp.maximum(t, 0.0) + jnp.log2(1.0 + jnp.exp2(-jnp.abs(t)))) * LN2


def _rms(x, eps):
    return x * lax.rsqrt(jnp.mean(x * x, axis=-1, keepdims=True) + eps)


def _iota(shape, dim):
    return lax.broadcasted_iota(jnp.int32, shape, dim)


def _constants():
    n = CUMSUM_BLOCK
    assert n == RWKV_WIDTH and CHUNK == HEAD_DIM
    r, c = _iota((n, n), 0), _iota((n, n), 1)
    same_chunk = (r >> 6) == (c >> 6)
    q_i = _iota((CHUNK, 128), 0)
    s_i = _iota((CHUNK, 128), 1) & (CHUNK - 1)
    return dict(
        tril=jnp.where(same_chunk & (c <= r), 1.0, 0.0).astype(BF16),
        head_blk=same_chunk,
        head_ones=jnp.where(same_chunk, 1.0, 0.0).astype(BF16),
        head_avg=jnp.where(same_chunk, 1.0 / HEAD_DIM, 0.0).astype(BF16),
        eye2=q_i == s_i, causal2=s_i <= q_i,
        head_spread=jnp.where(_iota((128, SSM_WIDTH), 0) == (_iota((128, SSM_WIDTH), 1) >> 6), 1.0, 0.0).astype(BF16),
        pair_blk=jnp.where((_iota((128, 128), 0) >> 6) == (_iota((128, 128), 1) >> 6), 1.0, 0.0).astype(BF16))


def _chunk_cumsum(slabs, tril):
    n = CHUNK * len(slabs)
    out = _dot_split_rhs(tril[:n, :n], jnp.concatenate(slabs, axis=0))
    return [out[i * CHUNK:(i + 1) * CHUNK] for i in range(len(slabs))]


def _chunks(tb):
    return [slice(c * CHUNK, (c + 1) * CHUNK) for c in range(tb // CHUNK)]


def _keep_tail(buf_ref):
    tb = buf_ref.shape[0] - 8
    buf_ref[0:8, :] = buf_ref[tb:tb + 8, :]


def _memkv_kernel(mem_ref, nw_ref, w_ref, o_ref):
    h = _rms(mem_ref[...], NORM_EPS) * nw_ref[...]
    o_ref[...] = _dot(h.astype(BF16), w_ref[...])


def _memkv(mem2d, nw, w_bf16):
    return pl.pallas_call(
        _memkv_kernel,
        out_shape=jax.ShapeDtypeStruct((MEM_LEN, 2 * XATTN_WIDTH), F32),
        name="memkv",
    )(mem2d, nw, w_bf16)


def _ssd_prep(pre_ref, zdt_ref, rows, cw_ref, cb_ref, dtb_ref, alog_ref):
    lo = 8 + rows.start
    conv = cb_ref[...] + pre_ref[lo:lo + CHUNK, :] * cw_ref[CONV_WIDTH - 1:CONV_WIDTH, :]
    for j in range(1, CONV_WIDTH):
        conv = conv + pre_ref[lo - j:lo - j + CHUNK, :] * cw_ref[CONV_WIDTH - 1 - j:CONV_WIDTH - j, :]
    xbc = _silu(conv)
    dt = _softplus(zdt_ref[rows, SSM_WIDTH:] + dtb_ref[...])
    return dict(xs=xbc[:, :SSM_WIDTH],
                bm=xbc[:, SSM_WIDTH:SSM_WIDTH + 256].astype(BF16),
                cm=xbc[:, SSM_WIDTH + 256:].astype(BF16),
                dt=dt, da=dt * (-jnp.exp(alog_ref[...])),
                zg=_silu(zdt_ref[rows, :SSM_WIDTH]))


def _ssd_spread(prep, consts):
    n = len(prep)
    a_cs = _chunk_cumsum([p["da"] for p in prep], consts["tril"])
    both = jnp.concatenate([p["dt"] for p in prep] + a_cs, axis=0)
    wide = _dot_split_lhs(both, consts["head_spread"])
    for i, p in enumerate(prep):
        p["xdt"] = p["xs"] * wide[i * CHUNK:(i + 1) * CHUNK]
    return [wide[(n + i) * CHUNK:(n + i + 1) * CHUNK] for i in range(n)]


_SSD_GROUPS = [slice(g * 256, (g + 1) * 256) for g in range(SSM_GROUPS)]
_SSD_GROUP_STATE = [slice(g * SSM_STATE, (g + 1) * SSM_STATE) for g in range(SSM_GROUPS)]


def _ssd_intra(p, acs_c, consts):
    eye2, causal2, bdmask = consts["eye2"], consts["causal2"], consts["pair_blk"]
    gl = _SSD_GROUP_STATE
    gps = [_dot_nt(p["cm"][:, gl[g]], jnp.concatenate([p["bm"][:, gl[g]]] * 2, axis=0))
           for g in range(SSM_GROUPS)]
    a_last = acs_c[CHUNK - 1:CHUNK, :]
    xend_c = (p["xdt"] * jnp.exp(a_last - acs_c)).astype(BF16)
    parts = []
    for j in range(SSM_HEADS // 2):
        lanes = slice(j * 128, (j + 1) * 128)
        colp = acs_c[:, lanes]
        rowp = jnp.sum(jnp.where(eye2, colp, 0.0), axis=0, keepdims=True)
        dm = jnp.exp(jnp.where(causal2, colp - rowp, -1e30))
        m = (gps[j // 2] * dm).astype(BF16)
        xd = p["xdt"][:, lanes]
        xd = xd.astype(BF16)
        bd = jnp.concatenate([xd, xd], axis=0) * bdmask
        parts.append(_dot(m, bd))
    incs = [_dot_tn(p["bm"][:, gl[g]], xend_c[:, _SSD_GROUPS[g]]) for g in range(SSM_GROUPS)]
    return jnp.concatenate(parts, axis=1), incs, jnp.exp(a_last)


def _ssd_state_stage(prep, a_cs, intra, dsk_ref, nw_ref, state_ref, y_ref):
    groups, gl = _SSD_GROUPS, _SSD_GROUP_STATE
    st = [state_ref[:, groups[g]] for g in range(SSM_GROUPS)]
    for c, p in enumerate(prep):
        y_diag, incs, decay = intra[c]
        y_off = jnp.concatenate([_dot(p["cm"][:, gl[g]], st[g].astype(BF16)) for g in range(SSM_GROUPS)],
                                axis=1)
        y = (y_diag + y_off * jnp.exp(a_cs[c]) + dsk_ref[...] * p["xs"]) * p["zg"]
        y_ref[c * CHUNK:(c + 1) * CHUNK, :] = y
        st = [st[g] * decay[:, groups[g]] + incs[g] for g in range(SSM_GROUPS)]
    for g in range(SSM_GROUPS):
        state_ref[:, groups[g]] = st[g]

    y = y_ref[...]
    y = jnp.concatenate([_rms(y[:, groups[g]], NORM_EPS) for g in range(SSM_GROUPS)], axis=1)
    return (y * nw_ref[...]).astype(BF16)


def _rwkv_prep(u_ref, rows, mu_ref, w2a_ref, w0_ref, a0_ref, kk_ref, ka_ref, consts):
    W = RWKV_WIDTH
    lo = 8 + rows.start
    u = u_ref[lo:lo + CHUNK, :]
    prev = u_ref[lo - 1:lo - 1 + CHUNK, :]
    us = u + (prev - u) * mu_ref[...]
    k = us[:, W:2 * W]
    lat = us[:, 4 * W:]
    lat = jnp.where(_iota(lat.shape, 1) < LORA, jnp.tanh(lat), lat)
    lora = _dot(lat.astype(BF16), w2a_ref[...])
    w_log = -_softplus(-(w0_ref[...] + lora[:, :W])) - 0.5
    a = _sigmoid(a0_ref[...] + lora[:, W:])
    kkr = k * kk_ref[...]
    kk = kkr * lax.rsqrt(jnp.maximum(_dot((kkr * kkr).astype(BF16), consts["head_ones"]), L2_EPS * L2_EPS))
    return dict(r=us[:, :W], v=us[:, 2 * W:3 * W], g=us[:, 3 * W:4 * W],
                logw=-jnp.exp(w_log),
                kk=kk, k2=k * (1.0 + (a - 1.0) * ka_ref[...]), b=kk * a)


def _rwkv_stages(prep, cum, rk_ref, lw_ref, lb_ref, state_ref, y_ref, consts, side_work):
    W = RWKV_WIDTH
    blk, head_ones, head_avg = consts["head_blk"], consts["head_ones"], consts["head_avg"]
    t_i = _iota((CHUNK, W), 0)
    j_i = _iota((CHUNK, W), 1) & (CHUNK - 1)
    one_zero = lambda m: jnp.where(m, 1.0, 0.0).astype(BF16)
    strict = one_zero(j_i < t_i)
    incl = one_zero(j_i <= t_i)
    head_masks_rows = [one_zero((_iota((CHUNK, W), 1) >> 6) == h) for h in range(RWKV_HEADS)]
    eye_pair = jnp.where((_iota((CHUNK, 128), 1) & (CHUNK - 1)) == _iota((CHUNK, 128), 0), 1.0, 0.0)
    first_of_pair = jnp.where((_iota((CHUNK, W), 1) & 127) < CHUNK, 1.0, 0.0).astype(BF16)
    second_of_pair = 1.0 - first_of_pair
    zero_blk = jnp.zeros((CHUNK, W), BF16)

    def bd(x):
        return jnp.concatenate([x] * RWKV_HEADS, axis=0) * head_ones

    ats, rts, vbs, a_ak, a_rk, a_rb, p_pairs, t_pairs = [], [], [], [], [], [], [], []
    for p, cum_c in zip(prep, cum):
        e_n = jnp.exp(-cum_c)
        at = (-p["kk"] * jnp.exp(cum_c - p["logw"])).astype(BF16)
        rt = p["r"] * jnp.exp(cum_c)
        kt = (p["k2"] * e_n).astype(BF16)
        bt = (p["b"] * e_n).astype(BF16)
        ats.append(at)
        rts.append(rt)
        vbs.append(p["v"].astype(BF16))
        x_cat = jnp.concatenate([at, rt.astype(BF16)], axis=0)
        y_stack = jnp.concatenate([y * m for y in (kt, bt) for m in head_masks_rows], axis=0)
        aa = _dot_nt(x_cat, y_stack).astype(BF16)
        a_ak.append(aa[:CHUNK, :W] * strict)
        a_ab = aa[:CHUNK, W:] * strict
        a_rk.append(aa[CHUNK:, :W] * incl)
        a_rb.append(aa[CHUNK:, W:] * incl)
        p_pairs.append([a_ab[:, :128], a_ab[:, 128:]])
        t_pairs.append([eye_pair, eye_pair])

    for step in range(6):
        for c in range(len(p_pairs)):
            for h2 in range(RWKV_HEADS // 2):
                pb = p_pairs[c][h2]
                pt = jnp.concatenate([pb, t_pairs[c][h2].astype(BF16)], axis=1)
                rhs = jnp.concatenate([pt * first_of_pair, pt * second_of_pair], axis=0)
                res = _dot(pb, rhs)
                p_pairs[c][h2] = res[:, :128].astype(BF16)
                t_pairs[c][h2] = t_pairs[c][h2] + res[:, 128:]
        side_work(step)

    nc = len(prep)
    v_bds = [bd(vbs[c]) for c in range(nc)]
    avs = [_dot(a_ak[c], v_bds[c]).astype(BF16) for c in range(nc)]
    wus = [_dot(jnp.concatenate(t_pairs[c], axis=1).astype(BF16),
                jnp.concatenate([bd(ats[c]), bd(avs[c])], axis=1)).astype(BF16) for c in range(nc)]
    qs = [(rts[c] + _dot(a_rb[c], bd(wus[c][:, :W]))).astype(BF16) for c in range(nc)]
    y0s = [_dot(jnp.concatenate([a_rk[c], a_rb[c]], axis=1),
                jnp.concatenate([v_bds[c], bd(wus[c][:, W:])], axis=0)) for c in range(nc)]
    ps, ns, decays = [], [], []
    for c, (p, cum_c) in enumerate(zip(prep, cum)):
        cum_last = cum_c[CHUNK - 1:CHUNK, :]
        e_end = jnp.exp(cum_last - cum_c)
        decays.append(jnp.exp(cum_last))
        mn_lhs = jnp.concatenate([wus[c], jnp.concatenate([zero_blk, vbs[c]], axis=1)], axis=0)
        bk = jnp.concatenate([p["b"] * e_end, p["k2"] * e_end], axis=0).astype(BF16)
        mn = _dot_tn(mn_lhs, bk)
        ps.append(jnp.where(blk, mn[:W], 0.0).astype(BF16))
        ns.append(jnp.where(blk, mn[W:], 0.0))

    s_bd = state_ref[...]
    for c in range(len(prep)):
        sb = s_bd.astype(BF16)
        y_ref[c * CHUNK:(c + 1) * CHUNK, :] = _dot_nt(qs[c], sb) + y0s[c]
        s_bd = s_bd * decays[c] + _dot(sb, ps[c]) + ns[c]
    state_ref[...] = s_bd

    whole = lambda key: jnp.concatenate([p[key] for p in prep], axis=0)
    r, k2, v, g = whole("r"), whole("k2"), whole("v"), whole("g")
    y = y_ref[...]
    yc = y - _dot(y.astype(BF16), head_avg)
    var = _dot((yc * yc).astype(BF16), head_avg)
    yn = yc * lax.rsqrt(var + LNX_EPS) * lw_ref[...] + lb_ref[...]
    bonus = _dot((r * k2 * rk_ref[...]).astype(BF16), head_ones) * v
    return ((yn + bonus) * _silu(g)).astype(BF16)


def _xattn_block(u, kt_ref, vb_ref, ones_ref):
    s = _dot(u[:, :XATTN_WIDTH].astype(BF16), kt_ref[...])
    ps = []
    for h in range(XATTN_HEADS):
        sh = s[:, h * MEM_LEN:(h + 1) * MEM_LEN]
        ps.append(jnp.exp(sh - jnp.max(sh, axis=-1, keepdims=True)))
    p = jnp.concatenate(ps, axis=1).astype(BF16)
    o = _dot(p, vb_ref[...]) * (1.0 / _dot(p, ones_ref[...]))
    return (o * _silu(u[:, XATTN_WIDTH:])).astype(BF16)


def _inproj(x, prew_ref, wmain_ref, wdt_ref, wrest_ref, rw_u, ssd_pre, zdt, xa):
    tb = x.shape[0]
    hb = (_rms(x, NORM_EPS) * prew_ref[...]).astype(BF16)
    rw_u[8:8 + tb, :] = _dot(hb, wrest_ref[:, :RWKV_IN])
    ssd_pre[8:8 + tb, :] = _dot(hb, wmain_ref[:, :CONV_DIM])
    zdt[:, :SSM_WIDTH] = _dot(hb, wmain_ref[:, CONV_DIM:])
    zdt[:, SSM_WIDTH:] = _dot(hb, wdt_ref[...])
    xa[...] = _dot(hb, wrest_ref[:, RWKV_IN:])


def _layer_kernel(x_ref, prew_ref, wmain_ref, wdt_ref, wrest_ref,
                  cw_ref, cb_ref, dtb_ref, alog_ref, dsk_ref, snw_ref,
                  mu_ref, w2a_ref, w0_ref, a0_ref, kk_ref, ka_ref, rk_ref, lw_ref, lb_ref,
                  kt_ref, vb_ref, ones_ref, wout_ref, postw_ref,
                  o_ref,
                  rw_u, ssd_pre, zdt, xa, ssd_state, ssd_y, rw_state, rw_y):
    tb = x_ref.shape[0]

    @pl.when(pl.program_id(0) == 0)
    def _():
        rw_u[0:8, :] = jnp.zeros((8, RWKV_IN), F32)
        ssd_pre[0:8, :] = jnp.zeros((8, CONV_DIM), F32)
        ssd_state[...] = jnp.zeros_like(ssd_state)
        rw_state[...] = jnp.zeros_like(rw_state)

    _inproj(x_ref[...], prew_ref, wmain_ref, wdt_ref, wrest_ref, rw_u, ssd_pre, zdt, xa)
    y_mem = _xattn_block(xa[...], kt_ref, vb_ref, ones_ref)

    consts = _constants()
    rw_prep, sd_prep, cum, a_cs = [], [], [], []
    for rows in _chunks(tb):
        rw_prep.append(_rwkv_prep(rw_u, rows, mu_ref, w2a_ref, w0_ref, a0_ref, kk_ref, ka_ref, consts))
        sd_prep.append(_ssd_prep(ssd_pre, zdt, rows, cw_ref, cb_ref, dtb_ref, alog_ref))
        done = len(rw_prep)
        if (done * CHUNK) % CUMSUM_BLOCK == 0 or done * CHUNK == tb:
            cum += _chunk_cumsum([p["logw"] for p in rw_prep[len(cum):]], consts["tril"])
            a_cs += _ssd_spread(sd_prep[len(a_cs):], consts)
    _keep_tail(rw_u)
    _keep_tail(ssd_pre)
    intra = []

    def ssd_side_work(step, steps=6):
        upto = -(-(step + 1) * len(sd_prep) // steps)
        while len(intra) < upto:
            intra.append(_ssd_intra(sd_prep[len(intra)], a_cs[len(intra)], consts))

    y_rw = _rwkv_stages(rw_prep, cum, rk_ref, lw_ref, lb_ref, rw_state, rw_y, consts, ssd_side_work)
    assert len(intra) == len(sd_prep)
    y_ssm = _ssd_state_stage(sd_prep, a_cs, intra, dsk_ref, snw_ref, ssd_state, ssd_y)
    d = _dot(jnp.concatenate([y_ssm, y_rw, y_mem], axis=1), wout_ref[...])
    o_ref[...] = x_ref[...] + _rms(d, NORM_EPS) * postw_ref[...]


def _layer(x2d, params):
    t = x2d.shape[0]
    tb = min(TB, t)
    assert t % tb == 0 and tb % CHUNK == 0
    rows = lambda i: (i, 0)
    whole = lambda a: pl.BlockSpec(a.shape, lambda i: (0,) * a.ndim)
    return pl.pallas_call(
        _layer_kernel,
        grid=(t // tb,),
        in_specs=[pl.BlockSpec((tb, D_MODEL), rows)] + [whole(p) for p in params],
        out_specs=pl.BlockSpec((tb, D_MODEL), rows),
        out_shape=jax.ShapeDtypeStruct((t, D_MODEL), F32),
        scratch_shapes=[pltpu.VMEM((tb + 8, RWKV_IN), F32),
                        pltpu.VMEM((tb + 8, CONV_DIM), F32),
                        pltpu.VMEM((tb, SSM_WIDTH + 128), F32),
                        pltpu.VMEM((tb, XATTN_IN), F32),
                        pltpu.VMEM((SSM_STATE, SSM_WIDTH), F32),
                        pltpu.VMEM((tb, SSM_WIDTH), F32),
                        pltpu.VMEM((RWKV_WIDTH, RWKV_WIDTH), F32),
                        pltpu.VMEM((tb, RWKV_WIDTH), F32)],
        compiler_params=pltpu.CompilerParams(dimension_semantics=("arbitrary",),
                                             vmem_limit_bytes=VMEM_LIMIT),
        name="layer",
    )(x2d, *params)


def _block_diag_heads(blocks):
    h, r, c = blocks.shape
    eye = jnp.eye(h, dtype=blocks.dtype)
    return (eye[:, None, :, None] * blocks[:, :, None, :]).reshape(h * r, h * c)


def kernel(x, mem, mem_norm_w, w_mem_kv, pre_norm_w, w_in, conv_w, conv_b, dt_bias, a_log, d_skip,
           ssm_norm_w, shift_mu, w0, w2, a0, a2, k_k, k_a, r_k, lnx_w, lnx_b, w_out, post_norm_w):
    assert x.shape[0] == 1 and mem.shape[0] == 1
    rep = lambda p: jnp.repeat(p, HEAD_DIM, axis=-1)
    row = lambda p: p.reshape(1, -1)

    kv = _memkv(mem[0], row(mem_norm_w), w_mem_kv.astype(BF16))
    mk = kv[:, :XATTN_WIDTH].reshape(MEM_LEN, XATTN_HEADS, HEAD_DIM)
    mv = kv[:, XATTN_WIDTH:].reshape(MEM_LEN, XATTN_HEADS, HEAD_DIM)
    kt_bd = _block_diag_heads(jnp.transpose(mk, (1, 2, 0)) * (HEAD_DIM ** -0.5)).astype(BF16)
    v_bd = _block_diag_heads(jnp.transpose(mv, (1, 0, 2))).astype(BF16)
    ones_bd = _block_diag_heads(jnp.ones((XATTN_HEADS, MEM_LEN, HEAD_DIM), BF16))

    pad8 = lambda p: jnp.pad(p, [(0, 0)] * (p.ndim - 1) + [(0, 128 - SSM_HEADS)])
    w_main = w_in[:, :, :CONV_DIM + SSM_WIDTH].astype(BF16)
    w_dt = pad8(w_in[:, :, CONV_DIM + SSM_WIDTH:SSM_IN].astype(BF16))
    w_rest = w_in[:, :, SSM_IN:].astype(BF16)
    w_out_k = w_out.astype(BF16)
    zeros = jnp.zeros((DEPTH, LORA, RWKV_WIDTH), F32)
    w2a = jnp.concatenate([jnp.concatenate([w2, zeros], axis=-1),
                           jnp.concatenate([zeros, a2], axis=-1)], axis=1).astype(BF16)

    xc = x[0]
    for i in range(DEPTH):
        params = (row(pre_norm_w[i]), w_main[i], w_dt[i], w_rest[i],
                  conv_w[i], row(conv_b[i]), row(pad8(dt_bias[i])), row(pad8(a_log[i])),
                  row(rep(d_skip[i])), row(ssm_norm_w[i]),
                  row(shift_mu[i]), w2a[i], row(w0[i]), row(a0[i]), row(k_k[i]), row(k_a[i]),
                  row(r_k[i]), row(lnx_w[i]), row(lnx_b[i]),
                  kt_bd, v_bd, ones_bd, w_out_k[i], row(post_norm_w[i]))
        xc = _layer(xc, params)
    return xc[None]
```

```python
import functools

import jax
import jax.numpy as jnp
from jax import lax
from jax.experimental import pallas as pl
from jax.experimental.pallas import tpu as pltpu

F32 = jnp.float32
BF16 = jnp.bfloat16

D_MODEL = 1024
DEPTH = 4
HEAD_DIM = 64
CHUNK = 64
MEM_LEN = 256
SSM_WIDTH = 512
SSM_HEADS = 8
SSM_GROUPS = 2
SSM_STATE = 128
CONV_WIDTH = 4
CONV_DIM = SSM_WIDTH + 2 * SSM_GROUPS * SSM_STATE
SSM_IN = CONV_DIM + SSM_WIDTH + SSM_HEADS
RWKV_WIDTH = 256
RWKV_HEADS = 4
LORA = 64
RWKV_IN = 4 * RWKV_WIDTH + 2 * LORA
XATTN_WIDTH = 256
XATTN_HEADS = 4
XATTN_IN = 2 * XATTN_WIDTH
NORM_EPS = 1e-6
LNX_EPS = 64e-5
L2_EPS = 1e-12

TB = 512
CUMSUM_BLOCK = 256
VMEM_LIMIT = 56 * 1024 * 1024


_dot = functools.partial(jnp.dot, preferred_element_type=F32)
_dot_nt = functools.partial(lax.dot_general, dimension_numbers=(((1,), (1,)), ((), ())),
                            preferred_element_type=F32)
_dot_tn = functools.partial(lax.dot_general, dimension_numbers=(((0,), (0,)), ((), ())),
                            preferred_element_type=F32)


def _dot_split_rhs(lhs_bf16, x):
    hi = x.astype(BF16)
    lo = (x - hi.astype(F32)).astype(BF16)
    return _dot(lhs_bf16, lo) + _dot(lhs_bf16, hi)


def _dot_split_lhs(x, rhs_bf16):
    hi = x.astype(BF16)
    lo = (x - hi.astype(F32)).astype(BF16)
    return _dot(lo, rhs_bf16) + _dot(hi, rhs_bf16)


def _sigmoid(x):
    return 1.0 / (1.0 + jnp.exp(-x))


def _silu(x):
    return x * _sigmoid(x)


LOG2E = 1.4426950408889634
LN2 = 0.6931471805599453


def _softplus(x):
    t = x * LOG2E
    return (jnp.maximum(t, 0.0) + jnp.log2(1.0 + jnp.exp2(-jnp.abs(t)))) * LN2


def _rms(x, eps):
    return x * lax.rsqrt(jnp.mean(x * x, axis=-1, keepdims=True) + eps)


def _iota(shape, dim):
    return lax.broadcasted_iota(jnp.int32, shape, dim)


def _constants():
    n = CUMSUM_BLOCK
    assert n == RWKV_WIDTH and CHUNK == HEAD_DIM
    r, c = _iota((n, n), 0), _iota((n, n), 1)
    same_chunk = (r >> 6) == (c >> 6)
    q_i = _iota((CHUNK, 128), 0)
    s_i = _iota((CHUNK, 128), 1) & (CHUNK - 1)
    return dict(
        tril=jnp.where(same_chunk & (c <= r), 1.0, 0.0).astype(BF16),
        head_blk=same_chunk,
        head_ones=jnp.where(same_chunk, 1.0, 0.0).astype(BF16),
        head_avg=jnp.where(same_chunk, 1.0 / HEAD_DIM, 0.0).astype(BF16),
        eye2=q_i == s_i, causal2=s_i <= q_i,
        head_spread=jnp.where(_iota((128, SSM_WIDTH), 0) == (_iota((128, SSM_WIDTH), 1) >> 6), 1.0, 0.0).astype(BF16),
        pair_blk=jnp.where((_iota((128, 128), 0) >> 6) == (_iota((128, 128), 1) >> 6), 1.0, 0.0).astype(BF16))


def _chunk_cumsum(slabs, tril):
    n = CHUNK * len(slabs)
    out = _dot_split_rhs(tril[:n, :n], jnp.concatenate(slabs, axis=0))
    return [out[i * CHUNK:(i + 1) * CHUNK] for i in range(len(slabs))]


def _chunks(tb):
    return [slice(c * CHUNK, (c + 1) * CHUNK) for c in range(tb // CHUNK)]


def _keep_tail(buf_ref):
    tb = buf_ref.shape[0] - 8
    buf_ref[0:8, :] = buf_ref[tb:tb + 8, :]


def _memkv_kernel(mem_ref, nw_ref, w_ref, o_ref):
    h = _rms(mem_ref[...], NORM_EPS) * nw_ref[...]
    o_ref[...] = _dot(h.astype(BF16), w_ref[...])


def _memkv(mem2d, nw, w_bf16):
    return pl.pallas_call(
        _memkv_kernel,
        out_shape=jax.ShapeDtypeStruct((MEM_LEN, 2 * XATTN_WIDTH), F32),
        name="memkv",
    )(mem2d, nw, w_bf16)


def _ssd_prep(pre_ref, zdt_ref, rows, cw_ref, cb_ref, dtb_ref, alog_ref):
    lo = 8 + rows.start
    conv = cb_ref[...] + pre_ref[lo:lo + CHUNK, :] * cw_ref[CONV_WIDTH - 1:CONV_WIDTH, :]
    for j in range(1, CONV_WIDTH):
        conv = conv + pre_ref[lo - j:lo - j + CHUNK, :] * cw_ref[CONV_WIDTH - 1 - j:CONV_WIDTH - j, :]
    xbc = _silu(conv)
    dt = _softplus(zdt_ref[rows, SSM_WIDTH:] + dtb_ref[...])
    return dict(xs=xbc[:, :SSM_WIDTH],
                bm=xbc[:, SSM_WIDTH:SSM_WIDTH + 256].astype(BF16),
                cm=xbc[:, SSM_WIDTH + 256:].astype(BF16),
                dt=dt, da=dt * (-jnp.exp(alog_ref[...])),
                zg=_silu(zdt_ref[rows, :SSM_WIDTH]))


def _ssd_spread(prep, consts):
    n = len(prep)
    a_cs = _chunk_cumsum([p["da"] for p in prep], consts["tril"])
    both = jnp.concatenate([p["dt"] for p in prep] + a_cs, axis=0)
    wide = _dot_split_lhs(both, consts["head_spread"])
    for i, p in enumerate(prep):
        p["xdt"] = p["xs"] * wide[i * CHUNK:(i + 1) * CHUNK]
    return [wide[(n + i) * CHUNK:(n + i + 1) * CHUNK] for i in range(n)]


_SSD_GROUPS = [slice(g * 256, (g + 1) * 256) for g in range(SSM_GROUPS)]
_SSD_GROUP_STATE = [slice(g * SSM_STATE, (g + 1) * SSM_STATE) for g in range(SSM_GROUPS)]


def _ssd_intra(p, acs_c, consts):
    eye2, causal2, bdmask = consts["eye2"], consts["causal2"], consts["pair_blk"]
    gl = _SSD_GROUP_STATE
    gps = [_dot_nt(p["cm"][:, gl[g]], jnp.concatenate([p["bm"][:, gl[g]]] * 2, axis=0))
           for g in range(SSM_GROUPS)]
    a_last = acs_c[CHUNK - 1:CHUNK, :]
    xend_c = (p["xdt"] * jnp.exp(a_last - acs_c)).astype(BF16)
    parts = []
    for j in range(SSM_HEADS // 2):
        lanes = slice(j * 128, (j + 1) * 128)
        colp = acs_c[:, lanes]
        rowp = jnp.sum(jnp.where(eye2, colp, 0.0), axis=0, keepdims=True)
        dm = jnp.exp(jnp.where(causal2, colp - rowp, -1e30))
        m = (gps[j // 2] * dm).astype(BF16)
        xd = p["xdt"][:, lanes]
        xd = xd.astype(BF16)
        bd = jnp.concatenate([xd, xd], axis=0) * bdmask
        parts.append(_dot(m, bd))
    incs = [_dot_tn(p["bm"][:, gl[g]], xend_c[:, _SSD_GROUPS[g]]) for g in range(SSM_GROUPS)]
    return jnp.concatenate(parts, axis=1), incs, jnp.exp(a_last)


def _ssd_state_stage(prep, a_cs, intra, dsk_ref, nw_ref, state_ref, y_ref):
    groups, gl = _SSD_GROUPS, _SSD_GROUP_STATE
    st = [state_ref[:, groups[g]] for g in range(SSM_GROUPS)]
    for c, p in enumerate(prep):
        y_diag, incs, decay = intra[c]
        y_off = jnp.concatenate([_dot(p["cm"][:, gl[g]], st[g].astype(BF16)) for g in range(SSM_GROUPS)],
                                axis=1)
        y = (y_diag + y_off * jnp.exp(a_cs[c]) + dsk_ref[...] * p["xs"]) * p["zg"]
        y_ref[c * CHUNK:(c + 1) * CHUNK, :] = y
        st = [st[g] * decay[:, groups[g]] + incs[g] for g in range(SSM_GROUPS)]
    for g in range(SSM_GROUPS):
        state_ref[:, groups[g]] = st[g]

    y = y_ref[...]
    y = jnp.concatenate([_rms(y[:, groups[g]], NORM_EPS) for g in range(SSM_GROUPS)], axis=1)
    return (y * nw_ref[...]).astype(BF16)


def _rwkv_prep(u_ref, rows, mu_ref, w2a_ref, w0_ref, a0_ref, kk_ref, ka_ref, consts):
    W = RWKV_WIDTH
    lo = 8 + rows.start
    u = u_ref[lo:lo + CHUNK, :]
    prev = u_ref[lo - 1:lo - 1 + CHUNK, :]
    us = u + (prev - u) * mu_ref[...]
    k = us[:, W:2 * W]
    lat = us[:, 4 * W:]
    lat = jnp.where(_iota(lat.shape, 1) < LORA, jnp.tanh(lat), lat)
    lora = _dot(lat.astype(BF16), w2a_ref[...])
    w_log = -_softplus(-(w0_ref[...] + lora[:, :W])) - 0.5
    a = _sigmoid(a0_ref[...] + lora[:, W:])
    kkr = k * kk_ref[...]
    kk = kkr * lax.rsqrt(jnp.maximum(_dot((kkr * kkr).astype(BF16), consts["head_ones"]), L2_EPS * L2_EPS))
    return dict(r=us[:, :W], v=us[:, 2 * W:3 * W], g=us[:, 3 * W:4 * W],
                logw=-jnp.exp(w_log),
                kk=kk, k2=k * (1.0 + (a - 1.0) * ka_ref[...]), b=kk * a)


def _rwkv_stages(prep, cum, rk_ref, lw_ref, lb_ref, state_ref, y_ref, consts, side_work):
    W = RWKV_WIDTH
    blk, head_ones, head_avg = consts["head_blk"], consts["head_ones"], consts["head_avg"]
    t_i = _iota((CHUNK, W), 0)
    j_i = _iota((CHUNK, W), 1) & (CHUNK - 1)
    one_zero = lambda m: jnp.where(m, 1.0, 0.0).astype(BF16)
    strict = one_zero(j_i < t_i)
    incl = one_zero(j_i <= t_i)
    head_masks_rows = [one_zero((_iota((CHUNK, W), 1) >> 6) == h) for h in range(RWKV_HEADS)]
    eye_pair = jnp.where((_iota((CHUNK, 128), 1) & (CHUNK - 1)) == _iota((CHUNK, 128), 0), 1.0, 0.0)
    first_of_pair = jnp.where((_iota((CHUNK, W), 1) & 127) < CHUNK, 1.0, 0.0).astype(BF16)
    second_of_pair = 1.0 - first_of_pair
    zero_blk = jnp.zeros((CHUNK, W), BF16)

    def bd(x):
        return jnp.concatenate([x] * RWKV_HEADS, axis=0) * head_ones

    ats, rts, vbs, a_ak, a_rk, a_rb, p_pairs, t_pairs = [], [], [], [], [], [], [], []
    for p, cum_c in zip(prep, cum):
        e_n = jnp.exp(-cum_c)
        at = (-p["kk"] * jnp.exp(cum_c - p["logw"])).astype(BF16)
        rt = p["r"] * jnp.exp(cum_c)
        kt = (p["k2"] * e_n).astype(BF16)
        bt = (p["b"] * e_n).astype(BF16)
        ats.append(at)
        rts.append(rt)
        vbs.append(p["v"].astype(BF16))
        x_cat = jnp.concatenate([at, rt.astype(BF16)], axis=0)
        y_stack = jnp.concatenate([y * m for y in (kt, bt) for m in head_masks_rows], axis=0)
        aa = _dot_nt(x_cat, y_stack).astype(BF16)
        a_ak.append(aa[:CHUNK, :W] * strict)
        a_ab = aa[:CHUNK, W:] * strict
        a_rk.append(aa[CHUNK:, :W] * incl)
        a_rb.append(aa[CHUNK:, W:] * incl)
        p_pairs.append([a_ab[:, :128], a_ab[:, 128:]])
        t_pairs.append([eye_pair, eye_pair])

    for step in range(6):
        for c in range(len(p_pairs)):
            for h2 in range(RWKV_HEADS // 2):
                pb = p_pairs[c][h2]
                pt = jnp.concatenate([pb, t_pairs[c][h2].astype(BF16)], axis=1)
                rhs = jnp.concatenate([pt * first_of_pair, pt * second_of_pair], axis=0)
                res = _dot(pb, rhs)
                p_pairs[c][h2] = res[:, :128].astype(BF16)
                t_pairs[c][h2] = t_pairs[c][h2] + res[:, 128:]
        side_work(step)

    nc = len(prep)
    v_bds = [bd(vbs[c]) for c in range(nc)]
    avs = [_dot(a_ak[c], v_bds[c]).astype(BF16) for c in range(nc)]
    wus = [_dot(jnp.concatenate(t_pairs[c], axis=1).astype(BF16),
                jnp.concatenate([bd(ats[c]), bd(avs[c])], axis=1)).astype(BF16) for c in range(nc)]
    qs = [(rts[c] + _dot(a_rb[c], bd(wus[c][:, :W]))).astype(BF16) for c in range(nc)]
    y0s = [_dot(jnp.concatenate([a_rk[c], a_rb[c]], axis=1),
                jnp.concatenate([v_bds[c], bd(wus[c][:, W:])], axis=0)) for c in range(nc)]
    ps, ns, decays = [], [], []
    for c, (p, cum_c) in enumerate(zip(prep, cum)):
        cum_last = cum_c[CHUNK - 1:CHUNK, :]
        e_end = jnp.exp(cum_last - cum_c)
        decays.append(jnp.exp(cum_last))
        mn_lhs = jnp.concatenate([wus[c], jnp.concatenate([zero_blk, vbs[c]], axis=1)], axis=0)
        bk = jnp.concatenate([p["b"] * e_end, p["k2"] * e_end], axis=0).astype(BF16)
        mn = _dot_tn(mn_lhs, bk)
        ps.append(jnp.where(blk, mn[:W], 0.0).astype(BF16))
        ns.append(jnp.where(blk, mn[W:], 0.0))

    s_bd = state_ref[...]
    for c in range(len(prep)):
        sb = s_bd.astype(BF16)
        y_ref[c * CHUNK:(c + 1) * CHUNK, :] = _dot_nt(qs[c], sb) + y0s[c]
        s_bd = s_bd * decays[c] + _dot(sb, ps[c]) + ns[c]
    state_ref[...] = s_bd

    whole = lambda key: jnp.concatenate([p[key] for p in prep], axis=0)
    r, k2, v, g = whole("r"), whole("k2"), whole("v"), whole("g")
    y = y_ref[...]
    yc = y - _dot(y.astype(BF16), head_avg)
    var = _dot((yc * yc).astype(BF16), head_avg)
    yn = yc * lax.rsqrt(var + LNX_EPS) * lw_ref[...] + lb_ref[...]
    bonus = _dot((r * k2 * rk_ref[...]).astype(BF16), head_ones) * v
    return ((yn + bonus) * _silu(g)).astype(BF16)


def _xattn_block(u, kt_ref, vb_ref, ones_ref):
    s = _dot(u[:, :XATTN_WIDTH].astype(BF16), kt_ref[...])
    ps = []
    for h in range(XATTN_HEADS):
        sh = s[:, h * MEM_LEN:(h + 1) * MEM_LEN]
        ps.append(jnp.exp(sh - jnp.max(sh, axis=-1, keepdims=True)))
    p = jnp.concatenate(ps, axis=1).astype(BF16)
    o = _dot(p, vb_ref[...]) * (1.0 / _dot(p, ones_ref[...]))
    return (o * _silu(u[:, XATTN_WIDTH:])).astype(BF16)


def _layer_kernel(x_ref, prew_ref, wmain_ref, wdt_ref, wrest_ref,
                  cw_ref, cb_ref, dtb_ref, alog_ref, dsk_ref, snw_ref,
                  mu_ref, w2a_ref, w0_ref, a0_ref, kk_ref, ka_ref, rk_ref, lw_ref, lb_ref,
                  kt_ref, vb_ref, ones_ref, wout_ref, postw_ref,
                  o_ref,
                  rw_u, ssd_pre, zdt, xa, ssd_state, ssd_y, rw_state, rw_y):
    tb = x_ref.shape[0]

    @pl.when(pl.program_id(0) == 0)
    def _():
        rw_u[0:8, :] = jnp.zeros((8, RWKV_IN), F32)
        ssd_pre[0:8, :] = jnp.zeros((8, CONV_DIM), F32)
        ssd_state[...] = jnp.zeros_like(ssd_state)
        rw_state[...] = jnp.zeros_like(rw_state)

    consts = _constants()
    hb = (_rms(x_ref[...], NORM_EPS) * prew_ref[...]).astype(BF16)
    rw_u[8:8 + tb, :] = _dot(hb, wrest_ref[:, :RWKV_IN])
    ssd_pre[8:8 + tb, :] = _dot(hb, wmain_ref[:, :CONV_DIM])
    rw_prep, sd_prep, cum, a_cs = [], [], [], []
    block_done = lambda n: (n * CHUNK) % CUMSUM_BLOCK == 0 or n * CHUNK == tb
    for rows in _chunks(tb):
        rw_prep.append(_rwkv_prep(rw_u, rows, mu_ref, w2a_ref, w0_ref, a0_ref, kk_ref, ka_ref, consts))
        if block_done(len(rw_prep)):
            cum += _chunk_cumsum([p["logw"] for p in rw_prep[len(cum):]], consts["tril"])
    zdt[:, :SSM_WIDTH] = _dot(hb, wmain_ref[:, CONV_DIM:])
    zdt[:, SSM_WIDTH:] = _dot(hb, wdt_ref[...])
    xa[...] = _dot(hb, wrest_ref[:, RWKV_IN:])
    y_mem = _xattn_block(xa[...], kt_ref, vb_ref, ones_ref)
    _keep_tail(rw_u)

    intra, queue = [], []

    def ssd_prep_slab(rows):
        sd_prep.append(_ssd_prep(ssd_pre, zdt, rows, cw_ref, cb_ref, dtb_ref, alog_ref))
        if block_done(len(sd_prep)):
            a_cs.extend(_ssd_spread(sd_prep[len(a_cs):], consts))

    for rows in _chunks(tb):
        queue.append(functools.partial(ssd_prep_slab, rows))
    for c in range(tb // CHUNK):
        queue.append(lambda c=c: intra.append(_ssd_intra(sd_prep[c], a_cs[c], consts)))
    total = len(queue)

    def ssd_side_work(step, steps=6):
        while len(queue) > total - -(-(step + 1) * total // steps):
            queue.pop(0)()

    y_rw = _rwkv_stages(rw_prep, cum, rk_ref, lw_ref, lb_ref, rw_state, rw_y, consts, ssd_side_work)
    assert not queue and len(intra) == len(sd_prep)
    _keep_tail(ssd_pre)
    y_ssm = _ssd_state_stage(sd_prep, a_cs, intra, dsk_ref, snw_ref, ssd_state, ssd_y)
    d = _dot(jnp.concatenate([y_ssm, y_rw, y_mem], axis=1), wout_ref[...])
    o_ref[...] = x_ref[...] + _rms(d, NORM_EPS) * postw_ref[...]


def _layer(x2d, params):
    t = x2d.shape[0]
    tb = min(TB, t)
    assert t % tb == 0 and tb % CHUNK == 0
    rows = lambda i: (i, 0)
    whole = lambda a: pl.BlockSpec(a.shape, lambda i: (0,) * a.ndim)
    return pl.pallas_call(
        _layer_kernel,
        grid=(t // tb,),
        in_specs=[pl.BlockSpec((tb, D_MODEL), rows)] + [whole(p) for p in params],
        out_specs=pl.BlockSpec((tb, D_MODEL), rows),
        out_shape=jax.ShapeDtypeStruct((t, D_MODEL), F32),
        scratch_shapes=[pltpu.VMEM((tb + 8, RWKV_IN), F32),
                        pltpu.VMEM((tb + 8, CONV_DIM), F32),
                        pltpu.VMEM((tb, SSM_WIDTH + 128), F32),
                        pltpu.VMEM((tb, XATTN_IN), F32),
                        pltpu.VMEM((SSM_STATE, SSM_WIDTH), F32),
                        pltpu.VMEM((tb, SSM_WIDTH), F32),
                        pltpu.VMEM((RWKV_WIDTH, RWKV_WIDTH), F32),
                        pltpu.VMEM((tb, RWKV_WIDTH), F32)],
        compiler_params=pltpu.CompilerParams(dimension_semantics=("arbitrary",),
                                             vmem_limit_bytes=VMEM_LIMIT),
        name="layer",
    )(x2d, *params)


def _block_diag_heads(blocks):
    h, r, c = blocks.shape
    eye = jnp.eye(h, dtype=blocks.dtype)
    return (eye[:, None, :, None] * blocks[:, :, None, :]).reshape(h * r, h * c)


def kernel(x, mem, mem_norm_w, w_mem_kv, pre_norm_w, w_in, conv_w, conv_b, dt_bias, a_log, d_skip,
           ssm_norm_w, shift_mu, w0, w2, a0, a2, k_k, k_a, r_k, lnx_w, lnx_b, w_out, post_norm_w):
    assert x.shape[0] == 1 and mem.shape[0] == 1
    rep = lambda p: jnp.repeat(p, HEAD_DIM, axis=-1)
    row = lambda p: p.reshape(1, -1)

    kv = _memkv(mem[0], row(mem_norm_w), w_mem_kv.astype(BF16))
    mk = kv[:, :XATTN_WIDTH].reshape(MEM_LEN, XATTN_HEADS, HEAD_DIM)
    mv = kv[:, XATTN_WIDTH:].reshape(MEM_LEN, XATTN_HEADS, HEAD_DIM)
    kt_bd = _block_diag_heads(jnp.transpose(mk, (1, 2, 0)) * (HEAD_DIM ** -0.5)).astype(BF16)
    v_bd = _block_diag_heads(jnp.transpose(mv, (1, 0, 2))).astype(BF16)
    ones_bd = _block_diag_heads(jnp.ones((XATTN_HEADS, MEM_LEN, HEAD_DIM), BF16))

    pad8 = lambda p: jnp.pad(p, [(0, 0)] * (p.ndim - 1) + [(0, 128 - SSM_HEADS)])
    w_in_b = w_in.astype(BF16)
    w_main = w_in_b[:, :, :CONV_DIM + SSM_WIDTH]
    w_dt = pad8(w_in_b[:, :, CONV_DIM + SSM_WIDTH:SSM_IN])
    w_rest = w_in_b[:, :, SSM_IN:]
    w_out_k = w_out.astype(BF16)
    zeros = jnp.zeros((DEPTH, LORA, RWKV_WIDTH), F32)
    w2a = jnp.concatenate([jnp.concatenate([w2, zeros], axis=-1),
                           jnp.concatenate([zeros, a2], axis=-1)], axis=1).astype(BF16)

    xc = x[0]
    for i in range(DEPTH):
        params = (row(pre_norm_w[i]), w_main[i], w_dt[i], w_rest[i],
                  conv_w[i], row(conv_b[i]), row(pad8(dt_bias[i])), row(pad8(a_log[i])),
                  row(rep(d_skip[i])), row(ssm_norm_w[i]),
                  row(shift_mu[i]), w2a[i], row(w0[i]), row(a0[i]), row(k_k[i]), row(k_a[i]),
                  row(r_k[i]), row(lnx_w[i]), row(lnx_b[i]),
                  kt_bd, v_bd, ones_bd, w_out_k[i], row(post_norm_w[i]))
        xc = _layer(xc, params)
    return xc[None]
```

```python
import functools

import jax
import jax.numpy as jnp
from jax import lax
from jax.experimental import pallas as pl
from jax.experimental.pallas import tpu as pltpu

F32 = jnp.float32
BF16 = jnp.bfloat16

D_MODEL = 1024
DEPTH = 4
HEAD_DIM = 64
CHUNK = 64
MEM_LEN = 256
SSM_WIDTH = 512
SSM_HEADS = 8
SSM_GROUPS = 2
SSM_STATE = 128
CONV_WIDTH = 4
CONV_DIM = SSM_WIDTH + 2 * SSM_GROUPS * SSM_STATE
SSM_IN = CONV_DIM + SSM_WIDTH + SSM_HEADS
RWKV_WIDTH = 256
RWKV_HEADS = 4
LORA = 64
RWKV_IN = 4 * RWKV_WIDTH + 2 * LORA
XATTN_WIDTH = 256
XATTN_HEADS = 4
XATTN_IN = 2 * XATTN_WIDTH
NORM_EPS = 1e-6
LNX_EPS = 64e-5
L2_EPS = 1e-12

TB = 512
CUMSUM_BLOCK = 256
VMEM_LIMIT = 56 * 1024 * 1024


_dot = functools.partial(jnp.dot, preferred_element_type=F32)
_dot_nt = functools.partial(lax.dot_general, dimension_numbers=(((1,), (1,)), ((), ())),
                            preferred_element_type=F32)
_dot_tn = functools.partial(lax.dot_general, dimension_numbers=(((0,), (0,)), ((), ())),
                            preferred_element_type=F32)


def _dot_split_rhs(lhs_bf16, x):
    hi = x.astype(BF16)
    lo = (x - hi.astype(F32)).astype(BF16)
    return _dot(lhs_bf16, lo) + _dot(lhs_bf16, hi)


def _dot_split_lhs(x, rhs_bf16):
    hi = x.astype(BF16)
    lo = (x - hi.astype(F32)).astype(BF16)
    return _dot(lo, rhs_bf16) + _dot(hi, rhs_bf16)


def _sigmoid(x):
    return 1.0 / (1.0 + jnp.exp(-x))


def _silu(x):
    return x * _sigmoid(x)


LOG2E = 1.4426950408889634
LN2 = 0.6931471805599453


def _softplus(x):
    t = x * LOG2E
    return (jnp.maximum(t, 0.0) + jnp.log2(1.0 + jnp.exp2(-jnp.abs(t)))) * LN2


def _rms(x, eps):
    return x * lax.rsqrt(jnp.mean(x * x, axis=-1, keepdims=True) + eps)


def _iota(shape, dim):
    return lax.broadcasted_iota(jnp.int32, shape, dim)


def _constants():
    n = CUMSUM_BLOCK
    assert n == RWKV_WIDTH and CHUNK == HEAD_DIM
    r, c = _iota((n, n), 0), _iota((n, n), 1)
    same_chunk = (r >> 6) == (c >> 6)
    q_i = _iota((CHUNK, 128), 0)
    s_i = _iota((CHUNK, 128), 1) & (CHUNK - 1)
    return dict(
        tril=jnp.where(same_chunk & (c <= r), 1.0, 0.0).astype(BF16),
        head_blk=same_chunk,
        head_ones=jnp.where(same_chunk, 1.0, 0.0).astype(BF16),
        head_avg=jnp.where(same_chunk, 1.0 / HEAD_DIM, 0.0).astype(BF16),
        eye2=q_i == s_i, causal2=s_i <= q_i,
        head_spread=jnp.where(_iota((128, SSM_WIDTH), 0) == (_iota((128, SSM_WIDTH), 1) >> 6), 1.0, 0.0).astype(BF16),
        pair_blk=jnp.where((_iota((128, 128), 0) >> 6) == (_iota((128, 128), 1) >> 6), 1.0, 0.0).astype(BF16))


def _chunk_cumsum(slabs, tril):
    n = CHUNK * len(slabs)
    out = _dot_split_rhs(tril[:n, :n], jnp.concatenate(slabs, axis=0))
    return [out[i * CHUNK:(i + 1) * CHUNK] for i in range(len(slabs))]


def _chunks(tb):
    return [slice(c * CHUNK, (c + 1) * CHUNK) for c in range(tb // CHUNK)]


def _keep_tail(buf_ref):
    tb = buf_ref.shape[0] - 8
    buf_ref[0:8, :] = buf_ref[tb:tb + 8, :]


def _memkv_kernel(mem_ref, nw_ref, w_ref, o_ref):
    h = _rms(mem_ref[...], NORM_EPS) * nw_ref[...]
    o_ref[...] = _dot(h.astype(BF16), w_ref[...])


def _memkv(mem2d, nw, w_bf16):
    return pl.pallas_call(
        _memkv_kernel,
        out_shape=jax.ShapeDtypeStruct((MEM_LEN, 2 * XATTN_WIDTH), F32),
        name="memkv",
    )(mem2d, nw, w_bf16)


def _ssd_prep(pre_ref, zdt_ref, rows, cw_ref, cb_ref, dtb_ref, alog_ref):
    lo = 8 + rows.start
    conv = cb_ref[...] + pre_ref[lo:lo + CHUNK, :] * cw_ref[CONV_WIDTH - 1:CONV_WIDTH, :]
    for j in range(1, CONV_WIDTH):
        conv = conv + pre_ref[lo - j:lo - j + CHUNK, :] * cw_ref[CONV_WIDTH - 1 - j:CONV_WIDTH - j, :]
    xbc = _silu(conv)
    dt = _softplus(zdt_ref[rows, SSM_WIDTH:] + dtb_ref[...])
    return dict(xs=xbc[:, :SSM_WIDTH],
                bm=xbc[:, SSM_WIDTH:SSM_WIDTH + 256].astype(BF16),
                cm=xbc[:, SSM_WIDTH + 256:].astype(BF16),
                dt=dt, da=dt * (-jnp.exp(alog_ref[...])),
                zg=_silu(zdt_ref[rows, :SSM_WIDTH]))


def _ssd_spread(prep, consts):
    n = len(prep)
    a_cs = _chunk_cumsum([p["da"] for p in prep], consts["tril"])
    dt_wide = _dot(jnp.concatenate([p["dt"] for p in prep], axis=0).astype(BF16), consts["head_spread"])
    acs_wide = _dot_split_lhs(jnp.concatenate(a_cs, axis=0), consts["head_spread"])
    for i, p in enumerate(prep):
        p["xdt"] = p["xs"] * dt_wide[i * CHUNK:(i + 1) * CHUNK]
    return [acs_wide[i * CHUNK:(i + 1) * CHUNK] for i in range(n)]


_SSD_GROUPS = [slice(g * 256, (g + 1) * 256) for g in range(SSM_GROUPS)]
_SSD_GROUP_STATE = [slice(g * SSM_STATE, (g + 1) * SSM_STATE) for g in range(SSM_GROUPS)]


def _ssd_intra(p, acs_c, consts):
    eye2, causal2, bdmask = consts["eye2"], consts["causal2"], consts["pair_blk"]
    gl = _SSD_GROUP_STATE
    gps = [_dot_nt(p["cm"][:, gl[g]], jnp.concatenate([p["bm"][:, gl[g]]] * 2, axis=0))
           for g in range(SSM_GROUPS)]
    a_last = acs_c[CHUNK - 1:CHUNK, :]
    xend_c = (p["xdt"] * jnp.exp(a_last - acs_c)).astype(BF16)
    parts = []
    for j in range(SSM_HEADS // 2):
        lanes = slice(j * 128, (j + 1) * 128)
        colp = acs_c[:, lanes]
        rowp = jnp.sum(jnp.where(eye2, colp, 0.0), axis=0, keepdims=True)
        dm = jnp.exp(jnp.where(causal2, colp - rowp, -1e30))
        m = (gps[j // 2] * dm).astype(BF16)
        xd = p["xdt"][:, lanes]
        xd = xd.astype(BF16)
        bd = jnp.concatenate([xd, xd], axis=0) * bdmask
        parts.append(_dot(m, bd))
    incs = [_dot_tn(p["bm"][:, gl[g]], xend_c[:, _SSD_GROUPS[g]]) for g in range(SSM_GROUPS)]
    return jnp.concatenate(parts, axis=1), incs, jnp.exp(a_last)


def _ssd_state_stage(prep, a_cs, intra, dsk_ref, nw_ref, state_ref, y_ref):
    groups, gl = _SSD_GROUPS, _SSD_GROUP_STATE
    st = [state_ref[:, groups[g]] for g in range(SSM_GROUPS)]
    for c, p in enumerate(prep):
        y_diag, incs, decay = intra[c]
        y_off = jnp.concatenate([_dot(p["cm"][:, gl[g]], st[g].astype(BF16)) for g in range(SSM_GROUPS)],
                                axis=1)
        y = (y_diag + y_off * jnp.exp(a_cs[c]) + dsk_ref[...] * p["xs"]) * p["zg"]
        y_ref[c * CHUNK:(c + 1) * CHUNK, :] = y
        st = [st[g] * decay[:, groups[g]] + incs[g] for g in range(SSM_GROUPS)]
    for g in range(SSM_GROUPS):
        state_ref[:, groups[g]] = st[g]

    y = y_ref[...]
    y = jnp.concatenate([_rms(y[:, groups[g]], NORM_EPS) for g in range(SSM_GROUPS)], axis=1)
    return (y * nw_ref[...]).astype(BF16)


def _rwkv_prep(u_ref, rows, mu_ref, w2a_ref, w0_ref, a0_ref, kk_ref, ka_ref, consts):
    W = RWKV_WIDTH
    lo = 8 + rows.start
    u = u_ref[lo:lo + CHUNK, :]
    prev = u_ref[lo - 1:lo - 1 + CHUNK, :]
    us = u + (prev - u) * mu_ref[...]
    k = us[:, W:2 * W]
    lat = us[:, 4 * W:]
    lat = jnp.where(_iota(lat.shape, 1) < LORA, jnp.tanh(lat), lat)
    lora = _dot(lat.astype(BF16), w2a_ref[...])
    w_log = -_softplus(-(w0_ref[...] + lora[:, :W])) - 0.5
    a = _sigmoid(a0_ref[...] + lora[:, W:])
    kkr = k * kk_ref[...]
    kk = kkr * lax.rsqrt(jnp.maximum(_dot((kkr * kkr).astype(BF16), consts["head_ones"]), L2_EPS * L2_EPS))
    return dict(r=us[:, :W], v=us[:, 2 * W:3 * W], g=us[:, 3 * W:4 * W],
                logw=-jnp.exp(w_log),
                kk=kk, k2=k * (1.0 + (a - 1.0) * ka_ref[...]), b=kk * a)


def _rwkv_stages(prep, cum, rk_ref, lw_ref, lb_ref, state_ref, y_ref, consts, side_work):
    W = RWKV_WIDTH
    blk, head_ones, head_avg = consts["head_blk"], consts["head_ones"], consts["head_avg"]
    t_i = _iota((CHUNK, W), 0)
    j_i = _iota((CHUNK, W), 1) & (CHUNK - 1)
    one_zero = lambda m: jnp.where(m, 1.0, 0.0).astype(BF16)
    strict = one_zero(j_i < t_i)
    incl = one_zero(j_i <= t_i)
    head_masks_rows = [one_zero((_iota((CHUNK, W), 1) >> 6) == h) for h in range(RWKV_HEADS)]
    eye_pair = jnp.where((_iota((CHUNK, 128), 1) & (CHUNK - 1)) == _iota((CHUNK, 128), 0), 1.0, 0.0)
    first_of_pair = jnp.where((_iota((CHUNK, W), 1) & 127) < CHUNK, 1.0, 0.0).astype(BF16)
    second_of_pair = 1.0 - first_of_pair
    zero_blk = jnp.zeros((CHUNK, W), BF16)

    def bd(x):
        return jnp.concatenate([x] * RWKV_HEADS, axis=0) * head_ones

    ats, rts, vbs, a_ak, a_rk, a_rb, p_pairs, t_pairs = [], [], [], [], [], [], [], []
    for p, cum_c in zip(prep, cum):
        e_n = jnp.exp(-cum_c)
        at = (-p["kk"] * jnp.exp(cum_c - p["logw"])).astype(BF16)
        rt = p["r"] * jnp.exp(cum_c)
        kt = (p["k2"] * e_n).astype(BF16)
        bt = (p["b"] * e_n).astype(BF16)
        ats.append(at)
        rts.append(rt)
        vbs.append(p["v"].astype(BF16))
        x_cat = jnp.concatenate([at, rt.astype(BF16)], axis=0)
        y_stack = jnp.concatenate([y * m for y in (kt, bt) for m in head_masks_rows], axis=0)
        aa = _dot_nt(x_cat, y_stack).astype(BF16)
        a_ak.append(aa[:CHUNK, :W] * strict)
        a_ab = aa[:CHUNK, W:] * strict
        a_rk.append(aa[CHUNK:, :W] * incl)
        a_rb.append(aa[CHUNK:, W:] * incl)
        p_pairs.append([a_ab[:, :128], a_ab[:, 128:]])
        t_pairs.append([eye_pair, eye_pair])

    for step in range(6):
        for c in range(len(p_pairs)):
            for h2 in range(RWKV_HEADS // 2):
                pb = p_pairs[c][h2]
                pt = jnp.concatenate([pb, t_pairs[c][h2].astype(BF16)], axis=1)
                rhs = jnp.concatenate([pt * first_of_pair, pt * second_of_pair], axis=0)
                res = _dot(pb, rhs)
                p_pairs[c][h2] = res[:, :128].astype(BF16)
                t_pairs[c][h2] = t_pairs[c][h2] + res[:, 128:]
        side_work(step)

    nc = len(prep)
    v_bds = [bd(vbs[c]) for c in range(nc)]
    avs = [_dot(a_ak[c], v_bds[c]).astype(BF16) for c in range(nc)]
    wus = [_dot(jnp.concatenate(t_pairs[c], axis=1).astype(BF16),
                jnp.concatenate([bd(ats[c]), bd(avs[c])], axis=1)).astype(BF16) for c in range(nc)]
    qs = [(rts[c] + _dot(a_rb[c], bd(wus[c][:, :W]))).astype(BF16) for c in range(nc)]
    y0s = [_dot(jnp.concatenate([a_rk[c], a_rb[c]], axis=1),
                jnp.concatenate([v_bds[c], bd(wus[c][:, W:])], axis=0)) for c in range(nc)]
    ps, ns, decays = [], [], []
    for c, (p, cum_c) in enumerate(zip(prep, cum)):
        cum_last = cum_c[CHUNK - 1:CHUNK, :]
        e_end = jnp.exp(cum_last - cum_c)
        decays.append(jnp.exp(cum_last))
        mn_lhs = jnp.concatenate([wus[c], jnp.concatenate([zero_blk, vbs[c]], axis=1)], axis=0)
        bk = jnp.concatenate([p["b"] * e_end, p["k2"] * e_end], axis=0).astype(BF16)
        mn = _dot_tn(mn_lhs, bk)
        ps.append(jnp.where(blk, mn[:W], 0.0).astype(BF16))
        ns.append(jnp.where(blk, mn[W:], 0.0))

    s_bd = state_ref[...]
    for c in range(len(prep)):
        sb = s_bd.astype(BF16)
        y_ref[c * CHUNK:(c + 1) * CHUNK, :] = _dot_nt(qs[c], sb) + y0s[c]
        s_bd = s_bd * decays[c] + _dot(sb, ps[c]) + ns[c]
    state_ref[...] = s_bd

    whole = lambda key: jnp.concatenate([p[key] for p in prep], axis=0)
    r, k2, v, g = whole("r"), whole("k2"), whole("v"), whole("g")
    y = y_ref[...]
    yc = y - _dot(y.astype(BF16), head_avg)
    var = _dot((yc * yc).astype(BF16), head_avg)
    yn = yc * lax.rsqrt(var + LNX_EPS) * lw_ref[...] + lb_ref[...]
    bonus = _dot((r * k2 * rk_ref[...]).astype(BF16), head_ones) * v
    return ((yn + bonus) * _silu(g)).astype(BF16)


def _xattn_block(u, kt_ref, vb_ref):
    s = _dot(u[:, :XATTN_WIDTH].astype(BF16), kt_ref[...])
    head = _iota((u.shape[0], XATTN_WIDTH), 1) >> 6
    ps, inv = [], None
    for h in range(XATTN_HEADS):
        sh = s[:, h * MEM_LEN:(h + 1) * MEM_LEN]
        ph = jnp.exp(sh - jnp.max(sh, axis=-1, keepdims=True))
        ps.append(ph)
        r = 1.0 / jnp.sum(ph, axis=-1, keepdims=True)
        inv = r if inv is None else jnp.where(head >= h, r, inv)
    o = _dot(jnp.concatenate(ps, axis=1).astype(BF16), vb_ref[...]) * inv
    return (o * _silu(u[:, XATTN_WIDTH:])).astype(BF16)


def _layer_kernel(x_ref, prew_ref, wmain_ref, wrest_ref,
                  cw_ref, cb_ref, dtb_ref, alog_ref, dsk_ref, snw_ref,
                  mu_ref, w2a_ref, w0_ref, a0_ref, kk_ref, ka_ref, rk_ref, lw_ref, lb_ref,
                  kt_ref, vb_ref, wout_ref, postw_ref,
                  o_ref,
                  rw_u, ssd_pre, zdt, xa, ssd_state, ssd_y, rw_state, rw_y):
    tb = x_ref.shape[0]

    @pl.when(pl.program_id(0) == 0)
    def _():
        rw_u[0:8, :] = jnp.zeros((8, RWKV_IN), F32)
        ssd_pre[0:8, :] = jnp.zeros((8, CONV_DIM), F32)
        ssd_state[...] = jnp.zeros_like(ssd_state)
        rw_state[...] = jnp.zeros_like(rw_state)

    consts = _constants()
    hb = (_rms(x_ref[...], NORM_EPS) * prew_ref[...]).astype(BF16)
    u_rw_dt = _dot(hb, wrest_ref[:, :RWKV_IN + 128])
    rw_u[8:8 + tb, :] = u_rw_dt[:, :RWKV_IN]
    zdt[:, SSM_WIDTH:] = u_rw_dt[:, RWKV_IN:]
    ssd_pre[8:8 + tb, :] = _dot(hb, wmain_ref[:, :CONV_DIM])
    rw_prep, sd_prep, cum, a_cs = [], [], [], []
    block_done = lambda n: (n * CHUNK) % CUMSUM_BLOCK == 0 or n * CHUNK == tb
    for rows in _chunks(tb):
        rw_prep.append(_rwkv_prep(rw_u, rows, mu_ref, w2a_ref, w0_ref, a0_ref, kk_ref, ka_ref, consts))
        if block_done(len(rw_prep)):
            cum += _chunk_cumsum([p["logw"] for p in rw_prep[len(cum):]], consts["tril"])
    zdt[:, :SSM_WIDTH] = _dot(hb, wmain_ref[:, CONV_DIM:])
    xa[...] = _dot(hb, wrest_ref[:, RWKV_IN + 128:])
    y_mem = _xattn_block(xa[...], kt_ref, vb_ref)
    _keep_tail(rw_u)

    intra, queue = [], []

    def ssd_prep_slab(rows):
        sd_prep.append(_ssd_prep(ssd_pre, zdt, rows, cw_ref, cb_ref, dtb_ref, alog_ref))
        if block_done(len(sd_prep)):
            a_cs.extend(_ssd_spread(sd_prep[len(a_cs):], consts))

    for rows in _chunks(tb):
        queue.append(functools.partial(ssd_prep_slab, rows))
    for c in range(tb // CHUNK):
        queue.append(lambda c=c: intra.append(_ssd_intra(sd_prep[c], a_cs[c], consts)))
    total = len(queue)

    def ssd_side_work(step, steps=6):
        while len(queue) > total - -(-(step + 1) * total // steps):
            queue.pop(0)()

    y_rw = _rwkv_stages(rw_prep, cum, rk_ref, lw_ref, lb_ref, rw_state, rw_y, consts, ssd_side_work)
    assert not queue and len(intra) == len(sd_prep)
    _keep_tail(ssd_pre)
    y_ssm = _ssd_state_stage(sd_prep, a_cs, intra, dsk_ref, snw_ref, ssd_state, ssd_y)
    d = _dot(jnp.concatenate([y_ssm, y_rw, y_mem], axis=1), wout_ref[...])
    o_ref[...] = x_ref[...] + _rms(d, NORM_EPS) * postw_ref[...]


def _layer(x2d, params):
    t = x2d.shape[0]
    tb = min(TB, t)
    assert t % tb == 0 and tb % CHUNK == 0
    rows = lambda i: (i, 0)
    whole = lambda a: pl.BlockSpec(a.shape, lambda i: (0,) * a.ndim)
    return pl.pallas_call(
        _layer_kernel,
        grid=(t // tb,),
        in_specs=[pl.BlockSpec((tb, D_MODEL), rows)] + [whole(p) for p in params],
        out_specs=pl.BlockSpec((tb, D_MODEL), rows),
        out_shape=jax.ShapeDtypeStruct((t, D_MODEL), F32),
        scratch_shapes=[pltpu.VMEM((tb + 8, RWKV_IN), F32),
                        pltpu.VMEM((tb + 8, CONV_DIM), F32),
                        pltpu.VMEM((tb, SSM_WIDTH + 128), F32),
                        pltpu.VMEM((tb, XATTN_IN), F32),
                        pltpu.VMEM((SSM_STATE, SSM_WIDTH), F32),
                        pltpu.VMEM((tb, SSM_WIDTH), F32),
                        pltpu.VMEM((RWKV_WIDTH, RWKV_WIDTH), F32),
                        pltpu.VMEM((tb, RWKV_WIDTH), F32)],
        compiler_params=pltpu.CompilerParams(dimension_semantics=("arbitrary",),
                                             vmem_limit_bytes=VMEM_LIMIT),
        name="layer",
    )(x2d, *params)


def _block_diag_heads(blocks):
    h, r, c = blocks.shape
    eye = jnp.eye(h, dtype=blocks.dtype)
    return (eye[:, None, :, None] * blocks[:, :, None, :]).reshape(h * r, h * c)


def kernel(x, mem, mem_norm_w, w_mem_kv, pre_norm_w, w_in, conv_w, conv_b, dt_bias, a_log, d_skip,
           ssm_norm_w, shift_mu, w0, w2, a0, a2, k_k, k_a, r_k, lnx_w, lnx_b, w_out, post_norm_w):
    assert x.shape[0] == 1 and mem.shape[0] == 1
    rep = lambda p: jnp.repeat(p, HEAD_DIM, axis=-1)
    row = lambda p: p.reshape(1, -1)

    kv = _memkv(mem[0], row(mem_norm_w), w_mem_kv.astype(BF16))
    mk = kv[:, :XATTN_WIDTH].reshape(MEM_LEN, XATTN_HEADS, HEAD_DIM)
    mv = kv[:, XATTN_WIDTH:].reshape(MEM_LEN, XATTN_HEADS, HEAD_DIM)
    kt_bd = _block_diag_heads(jnp.transpose(mk, (1, 2, 0)) * (HEAD_DIM ** -0.5)).astype(BF16)
    v_bd = _block_diag_heads(jnp.transpose(mv, (1, 0, 2))).astype(BF16)

    pad8 = lambda p: jnp.pad(p, [(0, 0)] * (p.ndim - 1) + [(0, 128 - SSM_HEADS)])
    w_in_b = w_in.astype(BF16)
    w_main = w_in_b[:, :, :CONV_DIM + SSM_WIDTH]
    w_rest = jnp.concatenate([w_in_b[:, :, SSM_IN:SSM_IN + RWKV_IN],
                              pad8(w_in_b[:, :, CONV_DIM + SSM_WIDTH:SSM_IN]),
                              w_in_b[:, :, SSM_IN + RWKV_IN:]], axis=-1)
    w_out_k = w_out.astype(BF16)
    zeros = jnp.zeros((DEPTH, LORA, RWKV_WIDTH), F32)
    w2a = jnp.concatenate([jnp.concatenate([w2, zeros], axis=-1),
                           jnp.concatenate([zeros, a2], axis=-1)], axis=1).astype(BF16)

    xc = x[0]
    for i in range(DEPTH):
        params = (row(pre_norm_w[i]), w_main[i], w_rest[i],
                  conv_w[i], row(conv_b[i]), row(pad8(dt_bias[i])), row(pad8(a_log[i])),
                  row(rep(d_skip[i])), row(ssm_norm_w[i]),
                  row(shift_mu[i]), w2a[i], row(w0[i]), row(a0[i]), row(k_k[i]), row(k_a[i]),
                  row(r_k[i]), row(lnx_w[i]), row(lnx_b[i]),
                  kt_bd, v_bd, w_out_k[i], row(post_norm_w[i]))
        xc = _layer(xc, params)
    return xc[None]
```

```python
import functools

import jax
import jax.numpy as jnp
from jax import lax
from jax.experimental import pallas as pl
from jax.experimental.pallas import tpu as pltpu

F32 = jnp.float32
BF16 = jnp.bfloat16

D_MODEL = 1024
DEPTH = 4
HEAD_DIM = 64
CHUNK = 64
MEM_LEN = 256
SSM_WIDTH = 512
SSM_HEADS = 8
SSM_GROUPS = 2
SSM_STATE = 128
CONV_WIDTH = 4
CONV_DIM = SSM_WIDTH + 2 * SSM_GROUPS * SSM_STATE
SSM_IN = CONV_DIM + SSM_WIDTH + SSM_HEADS
RWKV_WIDTH = 256
RWKV_HEADS = 4
LORA = 64
RWKV_IN = 4 * RWKV_WIDTH + 2 * LORA
XATTN_WIDTH = 256
XATTN_HEADS = 4
XATTN_IN = 2 * XATTN_WIDTH
NORM_EPS = 1e-6
LNX_EPS = 64e-5
L2_EPS = 1e-12

TB = 512
CUMSUM_BLOCK = 256
VMEM_LIMIT = 56 * 1024 * 1024


_dot = functools.partial(jnp.dot, preferred_element_type=F32)
_dot_nt = functools.partial(lax.dot_general, dimension_numbers=(((1,), (1,)), ((), ())),
                            preferred_element_type=F32)
_dot_tn = functools.partial(lax.dot_general, dimension_numbers=(((0,), (0,)), ((), ())),
                            preferred_element_type=F32)


def _dot_split_rhs(lhs_bf16, x):
    hi = x.astype(BF16)
    lo = (x - hi.astype(F32)).astype(BF16)
    return _dot(lhs_bf16, lo) + _dot(lhs_bf16, hi)


def _dot_split_lhs(x, rhs_bf16):
    hi = x.astype(BF16)
    lo = (x - hi.astype(F32)).astype(BF16)
    return _dot(lo, rhs_bf16) + _dot(hi, rhs_bf16)


def _sigmoid(x):
    return 1.0 / (1.0 + jnp.exp(-x))


def _silu(x):
    return x * _sigmoid(x)


LOG2E = 1.4426950408889634
LN2 = 0.6931471805599453


def _softplus(x):
    t = x * LOG2E
    return (jnp.maximum(t, 0.0) + jnp.log2(1.0 + jnp.exp2(-jnp.abs(t)))) * LN2


def _rms(x, eps):
    return x * lax.rsqrt(jnp.mean(x * x, axis=-1, keepdims=True) + eps)


def _iota(shape, dim):
    return lax.broadcasted_iota(jnp.int32, shape, dim)


def _constants():
    n = CUMSUM_BLOCK
    assert n == RWKV_WIDTH and CHUNK == HEAD_DIM
    r, c = _iota((n, n), 0), _iota((n, n), 1)
    same_chunk = (r >> 6) == (c >> 6)
    q_i = _iota((CHUNK, 256), 0)
    s_i = _iota((CHUNK, 256), 1) & (CHUNK - 1)
    return dict(
        tril=jnp.where(same_chunk & (c <= r), 1.0, 0.0).astype(BF16),
        head_blk=same_chunk,
        head_ones=jnp.where(same_chunk, 1.0, 0.0).astype(BF16),
        head_avg=jnp.where(same_chunk, 1.0 / HEAD_DIM, 0.0).astype(BF16),
        eye2=q_i == s_i, causal2=s_i <= q_i,
        head_spread=jnp.where(_iota((128, SSM_WIDTH), 0) == (_iota((128, SSM_WIDTH), 1) >> 6), 1.0, 0.0).astype(BF16),
        pair_blk=jnp.where((_iota((128, 128), 0) >> 6) == (_iota((128, 128), 1) >> 6), 1.0, 0.0).astype(BF16))


def _chunk_cumsum(slabs, tril):
    n = CHUNK * len(slabs)
    out = _dot_split_rhs(tril[:n, :n], jnp.concatenate(slabs, axis=0))
    return [out[i * CHUNK:(i + 1) * CHUNK] for i in range(len(slabs))]


def _chunks(tb):
    return [slice(c * CHUNK, (c + 1) * CHUNK) for c in range(tb // CHUNK)]


def _keep_tail(buf_ref):
    tb = buf_ref.shape[0] - 8
    buf_ref[0:8, :] = buf_ref[tb:tb + 8, :]


def _memkv_kernel(mem_ref, nw_ref, w_ref, o_ref):
    h = _rms(mem_ref[...], NORM_EPS) * nw_ref[...]
    o_ref[...] = _dot(h.astype(BF16), w_ref[...])


def _memkv(mem2d, nw, w_bf16):
    return pl.pallas_call(
        _memkv_kernel,
        out_shape=jax.ShapeDtypeStruct((MEM_LEN, 2 * XATTN_WIDTH), F32),
        name="memkv",
    )(mem2d, nw, w_bf16)


def _ssd_prep(pre_ref, zdt_ref, rows, cw_ref, cb_ref, dt_wide):
    lo = 8 + rows.start
    conv = cb_ref[...] + pre_ref[lo:lo + CHUNK, :] * cw_ref[CONV_WIDTH - 1:CONV_WIDTH, :]
    for j in range(1, CONV_WIDTH):
        conv = conv + pre_ref[lo - j:lo - j + CHUNK, :] * cw_ref[CONV_WIDTH - 1 - j:CONV_WIDTH - j, :]
    xbc = _silu(conv)
    xs = xbc[:, :SSM_WIDTH]
    return dict(xs=xs, xdt=xs * dt_wide,
                bm=xbc[:, SSM_WIDTH:SSM_WIDTH + 256].astype(BF16),
                cm=xbc[:, SSM_WIDTH + 256:].astype(BF16),
                zg=_silu(zdt_ref[rows, :SSM_WIDTH]))


_SSD_GROUPS = [slice(g * 256, (g + 1) * 256) for g in range(SSM_GROUPS)]
_SSD_GROUP_STATE = [slice(g * SSM_STATE, (g + 1) * SSM_STATE) for g in range(SSM_GROUPS)]


def _ssd_intra(p, acs_c, consts):
    eye4, causal4, bdmask = consts["eye2"], consts["causal2"], consts["pair_blk"]
    gl = _SSD_GROUP_STATE
    zero = jnp.zeros((128, 128), BF16)
    diag2 = lambda a, b: jnp.concatenate([jnp.concatenate([a, zero], axis=1),
                                          jnp.concatenate([zero, b], axis=1)], axis=0)
    b2 = [jnp.concatenate([p["bm"][:, gl[g]]] * 2, axis=0) for g in range(SSM_GROUPS)]
    gp = _dot_nt(p["cm"], diag2(b2[0], b2[1]))
    a_last = acs_c[CHUNK - 1:CHUNK, :]
    xend_c = (p["xdt"] * jnp.exp(a_last - acs_c)).astype(BF16)
    parts = []
    for g in range(SSM_GROUPS):
        lanes = _SSD_GROUPS[g]
        colp = acs_c[:, lanes]
        rowp = jnp.sum(jnp.where(eye4, colp, 0.0), axis=0, keepdims=True)
        dm = jnp.exp(jnp.where(causal4, colp - rowp, -1e30))
        gp_g = gp[:, g * 128:(g + 1) * 128]
        m = (jnp.concatenate([gp_g, gp_g], axis=1) * dm).astype(BF16)
        xd = p["xdt"][:, lanes].astype(BF16)
        bds = [jnp.concatenate([xd[:, j * 128:(j + 1) * 128]] * 2, axis=0) * bdmask for j in range(2)]
        parts.append(_dot(m, diag2(bds[0], bds[1])))
    incs = [_dot_tn(p["bm"][:, gl[g]], xend_c[:, _SSD_GROUPS[g]]) for g in range(SSM_GROUPS)]
    return jnp.concatenate(parts, axis=1), incs, jnp.exp(a_last)


def _ssd_state_stage(prep, a_cs, intra, dsk_ref, nw_ref, state_ref, y_ref):
    groups, gl = _SSD_GROUPS, _SSD_GROUP_STATE
    st = [state_ref[:, groups[g]] for g in range(SSM_GROUPS)]
    for c, p in enumerate(prep):
        y_diag, incs, decay = intra[c]
        y_off = jnp.concatenate([_dot(p["cm"][:, gl[g]], st[g].astype(BF16)) for g in range(SSM_GROUPS)],
                                axis=1)
        y = (y_diag + y_off * jnp.exp(a_cs[c]) + dsk_ref[...] * p["xs"]) * p["zg"]
        y_ref[c * CHUNK:(c + 1) * CHUNK, :] = y
        st = [st[g] * decay[:, groups[g]] + incs[g] for g in range(SSM_GROUPS)]
    for g in range(SSM_GROUPS):
        state_ref[:, groups[g]] = st[g]

    y = y_ref[...]
    y = jnp.concatenate([_rms(y[:, groups[g]], NORM_EPS) for g in range(SSM_GROUPS)], axis=1)
    return (y * nw_ref[...]).astype(BF16)


def _prep_block(u_ref, zdt_ref, slabs, mu_ref, w2a_ref, w0_ref, a0_ref, kk_ref, ka_ref, dtb_ref, alog_ref,
                consts):
    W = RWKV_WIDTH
    n = len(slabs)
    part, lats, squares, dts = [], [], [], []
    for rows in slabs:
        lo = 8 + rows.start
        u = u_ref[lo:lo + CHUNK, :]
        prev = u_ref[lo - 1:lo - 1 + CHUNK, :]
        us = u + (prev - u) * mu_ref[...]
        lat = us[:, 4 * W:]
        lats.append(jnp.where(_iota(lat.shape, 1) < LORA, jnp.tanh(lat), lat).astype(BF16))
        kkr = us[:, W:2 * W] * kk_ref[...]
        squares.append((kkr * kkr).astype(BF16))
        dts.append(_softplus(zdt_ref[rows, SSM_WIDTH:] + dtb_ref[...]))
        part.append((us, kkr))
    lora = _dot(jnp.concatenate(lats, axis=0), w2a_ref[...])
    ss = _dot(jnp.concatenate(squares, axis=0), consts["head_ones"])
    prep, logws = [], []
    for i, (us, kkr) in enumerate(part):
        rows = slice(i * CHUNK, (i + 1) * CHUNK)
        w_log = -_softplus(-(w0_ref[...] + lora[rows, :W])) - 0.5
        a = _sigmoid(a0_ref[...] + lora[rows, W:])
        k = us[:, W:2 * W]
        kk = kkr * lax.rsqrt(jnp.maximum(ss[rows], L2_EPS * L2_EPS))
        logws.append(-jnp.exp(w_log))
        prep.append(dict(r=us[:, :W], v=us[:, 2 * W:3 * W], g=us[:, 3 * W:4 * W], logw=logws[-1],
                         kk=kk, k2=k * (1.0 + (a - 1.0) * ka_ref[...]), b=kk * a))
    a_neg = -jnp.exp(alog_ref[...])
    sums = _chunk_cumsum([jnp.concatenate([lw, dt * a_neg], axis=1) for lw, dt in zip(logws, dts)],
                         consts["tril"])
    dt_wide = _dot(jnp.concatenate(dts, axis=0).astype(BF16), consts["head_spread"])
    acs_wide = _dot_split_lhs(jnp.concatenate([s[:, W:] for s in sums], axis=0), consts["head_spread"])
    cut = lambda x: [x[i * CHUNK:(i + 1) * CHUNK] for i in range(n)]
    return prep, [s[:, :W] for s in sums], cut(dt_wide), cut(acs_wide)


def _rwkv_stages(prep, cum, rk_ref, lw_ref, lb_ref, state_ref, y_ref, consts, side_work):
    W = RWKV_WIDTH
    blk, head_ones, head_avg = consts["head_blk"], consts["head_ones"], consts["head_avg"]
    t_i = _iota((CHUNK, W), 0)
    j_i = _iota((CHUNK, W), 1) & (CHUNK - 1)
    one_zero = lambda m: jnp.where(m, 1.0, 0.0).astype(BF16)
    strict = one_zero(j_i < t_i)
    incl = one_zero(j_i <= t_i)
    head_masks_rows = [one_zero((_iota((CHUNK, W), 1) >> 6) == h) for h in range(RWKV_HEADS)]
    eye_pair = jnp.where((_iota((CHUNK, 128), 1) & (CHUNK - 1)) == _iota((CHUNK, 128), 0), 1.0, 0.0)
    first_of_pair = jnp.where((_iota((CHUNK, W), 1) & 127) < CHUNK, 1.0, 0.0).astype(BF16)
    second_of_pair = 1.0 - first_of_pair
    zero_blk = jnp.zeros((CHUNK, W), BF16)

    def bd(x):
        return jnp.concatenate([x] * RWKV_HEADS, axis=0) * head_ones

    ats, rts, vbs, a_ak, a_rk, a_rb, p_pairs, t_pairs = [], [], [], [], [], [], [], []
    for p, cum_c in zip(prep, cum):
        e_n = jnp.exp(-cum_c)
        at = (-p["kk"] * jnp.exp(cum_c - p["logw"])).astype(BF16)
        rt = p["r"] * jnp.exp(cum_c)
        kt = (p["k2"] * e_n).astype(BF16)
        bt = (p["b"] * e_n).astype(BF16)
        ats.append(at)
        rts.append(rt)
        vbs.append(p["v"].astype(BF16))
        x_cat = jnp.concatenate([at, rt.astype(BF16)], axis=0)
        y_stack = jnp.concatenate([y * m for y in (kt, bt) for m in head_masks_rows], axis=0)
        aa = _dot_nt(x_cat, y_stack).astype(BF16)
        a_ak.append(aa[:CHUNK, :W] * strict)
        a_ab = aa[:CHUNK, W:] * strict
        a_rk.append(aa[CHUNK:, :W] * incl)
        a_rb.append(aa[CHUNK:, W:] * incl)
        p_pairs.append([a_ab[:, :128], a_ab[:, 128:]])
        t_pairs.append([eye_pair, eye_pair])

    for step in range(6):
        for c in range(len(p_pairs)):
            for h2 in range(RWKV_HEADS // 2):
                pb = p_pairs[c][h2]
                pt = jnp.concatenate([pb, t_pairs[c][h2].astype(BF16)], axis=1)
                rhs = jnp.concatenate([pt * first_of_pair, pt * second_of_pair], axis=0)
                res = _dot(pb, rhs)
                p_pairs[c][h2] = res[:, :128].astype(BF16)
                t_pairs[c][h2] = t_pairs[c][h2] + res[:, 128:]
        side_work(step)

    nc = len(prep)
    v_bds = [bd(vbs[c]) for c in range(nc)]
    avs = [_dot(a_ak[c], v_bds[c]).astype(BF16) for c in range(nc)]
    wus = [_dot(jnp.concatenate(t_pairs[c], axis=1).astype(BF16),
                jnp.concatenate([bd(ats[c]), bd(avs[c])], axis=1)).astype(BF16) for c in range(nc)]
    qs = [(rts[c] + _dot(a_rb[c], bd(wus[c][:, :W]))).astype(BF16) for c in range(nc)]
    y0s = [_dot(jnp.concatenate([a_rk[c], a_rb[c]], axis=1),
                jnp.concatenate([v_bds[c], bd(wus[c][:, W:])], axis=0)) for c in range(nc)]
    ps, ns, decays = [], [], []
    for c, (p, cum_c) in enumerate(zip(prep, cum)):
        cum_last = cum_c[CHUNK - 1:CHUNK, :]
        e_end = jnp.exp(cum_last - cum_c)
        decays.append(jnp.exp(cum_last))
        mn_lhs = jnp.concatenate([wus[c], jnp.concatenate([zero_blk, vbs[c]], axis=1)], axis=0)
        bk = jnp.concatenate([p["b"] * e_end, p["k2"] * e_end], axis=0).astype(BF16)
        mn = _dot_tn(mn_lhs, bk)
        ps.append(jnp.where(blk, mn[:W], 0.0).astype(BF16))
        ns.append(jnp.where(blk, mn[W:], 0.0))

    s_bd = state_ref[...]
    for c in range(len(prep)):
        sb = s_bd.astype(BF16)
        y_ref[c * CHUNK:(c + 1) * CHUNK, :] = _dot_nt(qs[c], sb) + y0s[c]
        s_bd = s_bd * decays[c] + _dot(sb, ps[c]) + ns[c]
    state_ref[...] = s_bd

    whole = lambda key: jnp.concatenate([p[key] for p in prep], axis=0)
    r, k2, v, g = whole("r"), whole("k2"), whole("v"), whole("g")
    y = y_ref[...]
    yc = y - _dot(y.astype(BF16), head_avg)
    var = _dot((yc * yc).astype(BF16), head_avg)
    yn = yc * lax.rsqrt(var + LNX_EPS) * lw_ref[...] + lb_ref[...]
    bonus = _dot((r * k2 * rk_ref[...]).astype(BF16), head_ones) * v
    return ((yn + bonus) * _silu(g)).astype(BF16)


def _xattn_block(u, kt_ref, vb_ref):
    s = _dot(u[:, :XATTN_WIDTH].astype(BF16), kt_ref[...])
    head = _iota((u.shape[0], XATTN_WIDTH), 1) >> 6
    ps, inv = [], None
    for h in range(XATTN_HEADS):
        sh = s[:, h * MEM_LEN:(h + 1) * MEM_LEN]
        ph = jnp.exp(sh - jnp.max(sh, axis=-1, keepdims=True))
        ps.append(ph)
        r = 1.0 / jnp.sum(ph, axis=-1, keepdims=True)
        inv = r if inv is None else jnp.where(head >= h, r, inv)
    o = _dot(jnp.concatenate(ps, axis=1).astype(BF16), vb_ref[...]) * inv
    return (o * _silu(u[:, XATTN_WIDTH:])).astype(BF16)


def _layer_kernel(x_ref, prew_ref, wmain_ref, wrest_ref,
                  cw_ref, cb_ref, dtb_ref, alog_ref, dsk_ref, snw_ref,
                  mu_ref, w2a_ref, w0_ref, a0_ref, kk_ref, ka_ref, rk_ref, lw_ref, lb_ref,
                  kt_ref, vb_ref, wout_ref, postw_ref,
                  o_ref,
                  rw_u, ssd_pre, zdt, xa, ssd_state, ssd_y, rw_state, rw_y):
    tb = x_ref.shape[0]

    @pl.when(pl.program_id(0) == 0)
    def _():
        rw_u[0:8, :] = jnp.zeros((8, RWKV_IN), F32)
        ssd_pre[0:8, :] = jnp.zeros((8, CONV_DIM), F32)
        ssd_state[...] = jnp.zeros_like(ssd_state)
        rw_state[...] = jnp.zeros_like(rw_state)

    consts = _constants()
    hb = (_rms(x_ref[...], NORM_EPS) * prew_ref[...]).astype(BF16)
    u_rw_dt = _dot(hb, wrest_ref[:, :RWKV_IN + 128])
    rw_u[8:8 + tb, :] = u_rw_dt[:, :RWKV_IN]
    zdt[:, SSM_WIDTH:] = u_rw_dt[:, RWKV_IN:]
    ssd_pre[8:8 + tb, :] = _dot(hb, wmain_ref[:, :CONV_DIM])
    rw_prep, sd_prep, cum, a_cs, dt_wide = [], [], [], [], []
    chunks = _chunks(tb)
    per_block = CUMSUM_BLOCK // CHUNK
    for i in range(0, len(chunks), per_block):
        block = _prep_block(rw_u, zdt, chunks[i:i + per_block], mu_ref, w2a_ref, w0_ref, a0_ref, kk_ref,
                            ka_ref, dtb_ref, alog_ref, consts)
        for acc, new in zip((rw_prep, cum, dt_wide, a_cs), block):
            acc.extend(new)
    zdt[:, :SSM_WIDTH] = _dot(hb, wmain_ref[:, CONV_DIM:])
    xa[...] = _dot(hb, wrest_ref[:, RWKV_IN + 128:])
    y_mem = _xattn_block(xa[...], kt_ref, vb_ref)
    _keep_tail(rw_u)

    intra, queue = [], []

    def ssd_prep_slab(c):
        sd_prep.append(_ssd_prep(ssd_pre, zdt, chunks[c], cw_ref, cb_ref, dt_wide[c]))

    for c in range(len(chunks)):
        queue.append(functools.partial(ssd_prep_slab, c))
    for c in range(tb // CHUNK):
        queue.append(lambda c=c: intra.append(_ssd_intra(sd_prep[c], a_cs[c], consts)))
    total = len(queue)

    def ssd_side_work(step, steps=6):
        while len(queue) > total - -(-(step + 1) * total // steps):
            queue.pop(0)()

    y_rw = _rwkv_stages(rw_prep, cum, rk_ref, lw_ref, lb_ref, rw_state, rw_y, consts, ssd_side_work)
    assert not queue and len(intra) == len(sd_prep)
    _keep_tail(ssd_pre)
    y_ssm = _ssd_state_stage(sd_prep, a_cs, intra, dsk_ref, snw_ref, ssd_state, ssd_y)
    d = _dot(jnp.concatenate([y_ssm, y_rw, y_mem], axis=1), wout_ref[...])
    o_ref[...] = x_ref[...] + _rms(d, NORM_EPS) * postw_ref[...]


def _layer(x2d, params):
    t = x2d.shape[0]
    tb = min(TB, t)
    assert t % tb == 0 and tb % CHUNK == 0
    rows = lambda i: (i, 0)
    whole = lambda a: pl.BlockSpec(a.shape, lambda i: (0,) * a.ndim)
    return pl.pallas_call(
        _layer_kernel,
        grid=(t // tb,),
        in_specs=[pl.BlockSpec((tb, D_MODEL), rows)] + [whole(p) for p in params],
        out_specs=pl.BlockSpec((tb, D_MODEL), rows),
        out_shape=jax.ShapeDtypeStruct((t, D_MODEL), F32),
        scratch_shapes=[pltpu.VMEM((tb + 8, RWKV_IN), F32),
                        pltpu.VMEM((tb + 8, CONV_DIM), F32),
                        pltpu.VMEM((tb, SSM_WIDTH + 128), F32),
                        pltpu.VMEM((tb, XATTN_IN), F32),
                        pltpu.VMEM((SSM_STATE, SSM_WIDTH), F32),
                        pltpu.VMEM((tb, SSM_WIDTH), F32),
                        pltpu.VMEM((RWKV_WIDTH, RWKV_WIDTH), F32),
                        pltpu.VMEM((tb, RWKV_WIDTH), F32)],
        compiler_params=pltpu.CompilerParams(dimension_semantics=("arbitrary",),
                                             vmem_limit_bytes=VMEM_LIMIT),
        name="layer",
    )(x2d, *params)


def _block_diag_heads(blocks):
    h, r, c = blocks.shape
    eye = jnp.eye(h, dtype=blocks.dtype)
    return (eye[:, None, :, None] * blocks[:, :, None, :]).reshape(h * r, h * c)


def kernel(x, mem, mem_norm_w, w_mem_kv, pre_norm_w, w_in, conv_w, conv_b, dt_bias, a_log, d_skip,
           ssm_norm_w, shift_mu, w0, w2, a0, a2, k_k, k_a, r_k, lnx_w, lnx_b, w_out, post_norm_w):
    assert x.shape[0] == 1 and mem.shape[0] == 1
    rep = lambda p: jnp.repeat(p, HEAD_DIM, axis=-1)
    row = lambda p: p.reshape(1, -1)

    kv = _memkv(mem[0], row(mem_norm_w), w_mem_kv.astype(BF16))
    mk = kv[:, :XATTN_WIDTH].reshape(MEM_LEN, XATTN_HEADS, HEAD_DIM)
    mv = kv[:, XATTN_WIDTH:].reshape(MEM_LEN, XATTN_HEADS, HEAD_DIM)
    kt_bd = _block_diag_heads(jnp.transpose(mk, (1, 2, 0)) * (HEAD_DIM ** -0.5)).astype(BF16)
    v_bd = _block_diag_heads(jnp.transpose(mv, (1, 0, 2))).astype(BF16)

    pad8 = lambda p: jnp.pad(p, [(0, 0)] * (p.ndim - 1) + [(0, 128 - SSM_HEADS)])
    w_in_b = w_in.astype(BF16)
    w_main = w_in_b[:, :, :CONV_DIM + SSM_WIDTH]
    w_rest = jnp.concatenate([w_in_b[:, :, SSM_IN:SSM_IN + RWKV_IN],
                              pad8(w_in_b[:, :, CONV_DIM + SSM_WIDTH:SSM_IN]),
                              w_in_b[:, :, SSM_IN + RWKV_IN:]], axis=-1)
    w_out_k = w_out.astype(BF16)
    zeros = jnp.zeros((DEPTH, LORA, RWKV_WIDTH), F32)
    w2a = jnp.concatenate([jnp.concatenate([w2, zeros], axis=-1),
                           jnp.concatenate([zeros, a2], axis=-1)], axis=1).astype(BF16)

    xc = x[0]
    for i in range(DEPTH):
        params = (row(pre_norm_w[i]), w_main[i], w_rest[i],
                  conv_w[i], row(conv_b[i]), row(pad8(dt_bias[i])), row(pad8(a_log[i])),
                  row(rep(d_skip[i])), row(ssm_norm_w[i]),
                  row(shift_mu[i]), w2a[i], row(w0[i]), row(a0[i]), row(k_k[i]), row(k_a[i]),
                  row(r_k[i]), row(lnx_w[i]), row(lnx_b[i]),
                  kt_bd, v_bd, w_out_k[i], row(post_norm_w[i]))
        xc = _layer(xc, params)
    return xc[None]
```

```python
import functools

import jax
import jax.numpy as jnp
from jax import lax
from jax.experimental import pallas as pl
from jax.experimental.pallas import tpu as pltpu

F32 = jnp.float32
BF16 = jnp.bfloat16

D_MODEL = 1024
DEPTH = 4
HEAD_DIM = 64
CHUNK = 64
MEM_LEN = 256
SSM_WIDTH = 512
SSM_HEADS = 8
SSM_GROUPS = 2
SSM_STATE = 128
CONV_WIDTH = 4
CONV_DIM = SSM_WIDTH + 2 * SSM_GROUPS * SSM_STATE
SSM_IN = CONV_DIM + SSM_WIDTH + SSM_HEADS
RWKV_WIDTH = 256
RWKV_HEADS = 4
LORA = 64
RWKV_IN = 4 * RWKV_WIDTH + 2 * LORA
XATTN_WIDTH = 256
XATTN_HEADS = 4
XATTN_IN = 2 * XATTN_WIDTH
NORM_EPS = 1e-6
LNX_EPS = 64e-5
L2_EPS = 1e-12

TB = 512
CUMSUM_BLOCK = 256
VMEM_LIMIT = 56 * 1024 * 1024


_dot = functools.partial(jnp.dot, preferred_element_type=F32)
_dot_nt = functools.partial(lax.dot_general, dimension_numbers=(((1,), (1,)), ((), ())),
                            preferred_element_type=F32)
_dot_tn = functools.partial(lax.dot_general, dimension_numbers=(((0,), (0,)), ((), ())),
                            preferred_element_type=F32)


def _dot_split_rhs(lhs_bf16, x):
    hi = x.astype(BF16)
    lo = (x - hi.astype(F32)).astype(BF16)
    return _dot(lhs_bf16, lo) + _dot(lhs_bf16, hi)


def _dot_split_lhs(x, rhs_bf16):
    hi = x.astype(BF16)
    lo = (x - hi.astype(F32)).astype(BF16)
    return _dot(lo, rhs_bf16) + _dot(hi, rhs_bf16)


def _sigmoid(x):
    return 1.0 / (1.0 + jnp.exp(-x))


def _silu(x):
    return x * _sigmoid(x)


LOG2E = 1.4426950408889634
LN2 = 0.6931471805599453


def _softplus(x):
    t = x * LOG2E
    return (jnp.maximum(t, 0.0) + jnp.log2(1.0 + jnp.exp2(-jnp.abs(t)))) * LN2


def _rms(x, eps):
    return x * lax.rsqrt(jnp.mean(x * x, axis=-1, keepdims=True) + eps)


def _iota(shape, dim):
    return lax.broadcasted_iota(jnp.int32, shape, dim)


def _constants():
    n = CUMSUM_BLOCK
    assert n == RWKV_WIDTH and CHUNK == HEAD_DIM
    r, c = _iota((n, n), 0), _iota((n, n), 1)
    same_chunk = (r >> 6) == (c >> 6)
    q_i = _iota((CHUNK, 256), 0)
    s_i = _iota((CHUNK, 256), 1) & (CHUNK - 1)
    return dict(
        tril=jnp.where(same_chunk & (c <= r), 1.0, 0.0).astype(BF16),
        head_blk=same_chunk,
        head_ones=jnp.where(same_chunk, 1.0, 0.0).astype(BF16),
        head_avg=jnp.where(same_chunk, 1.0 / HEAD_DIM, 0.0).astype(BF16),
        eye2=q_i == s_i, causal2=s_i <= q_i,
        head_spread=jnp.where(_iota((128, SSM_WIDTH), 0) == (_iota((128, SSM_WIDTH), 1) >> 6), 1.0, 0.0).astype(BF16),
        pair_blk=jnp.where((_iota((128, 128), 0) >> 6) == (_iota((128, 128), 1) >> 6), 1.0, 0.0).astype(BF16))


def _chunk_cumsum(slabs, tril):
    n = CHUNK * len(slabs)
    out = _dot_split_rhs(tril[:n, :n], jnp.concatenate(slabs, axis=0))
    return [out[i * CHUNK:(i + 1) * CHUNK] for i in range(len(slabs))]


def _chunks(tb):
    return [slice(c * CHUNK, (c + 1) * CHUNK) for c in range(tb // CHUNK)]


def _keep_tail(buf_ref):
    tb = buf_ref.shape[0] - 8
    buf_ref[0:8, :] = buf_ref[tb:tb + 8, :]


def _memkv_kernel(mem_ref, nw_ref, w_ref, o_ref):
    h = _rms(mem_ref[...], NORM_EPS) * nw_ref[...]
    o_ref[...] = _dot(h.astype(BF16), w_ref[...])


def _memkv(mem2d, nw, w_bf16):
    return pl.pallas_call(
        _memkv_kernel,
        out_shape=jax.ShapeDtypeStruct((MEM_LEN, 2 * XATTN_WIDTH), F32),
        name="memkv",
    )(mem2d, nw, w_bf16)


N_MAIN = CONV_DIM + SSM_WIDTH
N_REST = RWKV_IN + 128 + XATTN_IN


def _wprep_kernel(w_ref, main_ref, rest_ref):
    w = w_ref[0]
    main_ref[0] = w[:, :N_MAIN].astype(BF16)
    rest_ref[0, :, :RWKV_IN] = w[:, SSM_IN:SSM_IN + RWKV_IN].astype(BF16)
    dt_tile = w[:, N_MAIN:N_MAIN + 128]
    rest_ref[0, :, RWKV_IN:RWKV_IN + 128] = jnp.where(_iota(dt_tile.shape, 1) < SSM_HEADS, dt_tile, 0.0).astype(BF16)
    rest_ref[0, :, RWKV_IN + 128:] = w[:, SSM_IN + RWKV_IN:].astype(BF16)


def _wprep(w_in):
    depth, d, n = w_in.shape
    return pl.pallas_call(
        _wprep_kernel,
        grid=(depth,),
        in_specs=[pl.BlockSpec((1, d, n), lambda i: (i, 0, 0))],
        out_specs=[pl.BlockSpec((1, d, N_MAIN), lambda i: (i, 0, 0)),
                   pl.BlockSpec((1, d, N_REST), lambda i: (i, 0, 0))],
        out_shape=[jax.ShapeDtypeStruct((depth, d, N_MAIN), BF16),
                   jax.ShapeDtypeStruct((depth, d, N_REST), BF16)],
        compiler_params=pltpu.CompilerParams(dimension_semantics=("arbitrary",),
                                             vmem_limit_bytes=VMEM_LIMIT),
        name="wprep",
    )(w_in)


def _ssd_prep(pre_ref, zdt_ref, rows, cw_ref, cb_ref, dt_wide):
    lo = 8 + rows.start
    conv = cb_ref[...] + pre_ref[lo:lo + CHUNK, :] * cw_ref[CONV_WIDTH - 1:CONV_WIDTH, :]
    for j in range(1, CONV_WIDTH):
        conv = conv + pre_ref[lo - j:lo - j + CHUNK, :] * cw_ref[CONV_WIDTH - 1 - j:CONV_WIDTH - j, :]
    xbc = _silu(conv)
    xs = xbc[:, :SSM_WIDTH]
    return dict(xs=xs, xdt=xs * dt_wide,
                bm=xbc[:, SSM_WIDTH:SSM_WIDTH + 256].astype(BF16),
                cm=xbc[:, SSM_WIDTH + 256:].astype(BF16),
                zg=_silu(zdt_ref[rows, :SSM_WIDTH]))


_SSD_GROUPS = [slice(g * 256, (g + 1) * 256) for g in range(SSM_GROUPS)]
_SSD_GROUP_STATE = [slice(g * SSM_STATE, (g + 1) * SSM_STATE) for g in range(SSM_GROUPS)]


def _ssd_intra(p, acs_c, consts):
    eye4, causal4, bdmask = consts["eye2"], consts["causal2"], consts["pair_blk"]
    gl = _SSD_GROUP_STATE
    zero = jnp.zeros((128, 128), BF16)
    diag2 = lambda a, b: jnp.concatenate([jnp.concatenate([a, zero], axis=1),
                                          jnp.concatenate([zero, b], axis=1)], axis=0)
    b2 = [jnp.concatenate([p["bm"][:, gl[g]]] * 2, axis=0) for g in range(SSM_GROUPS)]
    gp = _dot_nt(p["cm"], diag2(b2[0], b2[1]))
    a_last = acs_c[CHUNK - 1:CHUNK, :]
    xend_c = (p["xdt"] * jnp.exp(a_last - acs_c)).astype(BF16)
    parts = []
    for g in range(SSM_GROUPS):
        lanes = _SSD_GROUPS[g]
        colp = acs_c[:, lanes]
        rowp = jnp.sum(jnp.where(eye4, colp, 0.0), axis=0, keepdims=True)
        dm = jnp.exp(jnp.where(causal4, colp - rowp, -1e30))
        gp_g = gp[:, g * 128:(g + 1) * 128]
        m = (jnp.concatenate([gp_g, gp_g], axis=1) * dm).astype(BF16)
        xd = p["xdt"][:, lanes].astype(BF16)
        bds = [jnp.concatenate([xd[:, j * 128:(j + 1) * 128]] * 2, axis=0) * bdmask for j in range(2)]
        parts.append(_dot(m, diag2(bds[0], bds[1])))
    incs = [_dot_tn(p["bm"][:, gl[g]], xend_c[:, _SSD_GROUPS[g]]) for g in range(SSM_GROUPS)]
    return jnp.concatenate(parts, axis=1), incs, jnp.exp(a_last)


def _ssd_state_stage(prep, a_cs, intra, dsk_ref, nw_ref, state_ref, y_ref):
    groups, gl = _SSD_GROUPS, _SSD_GROUP_STATE
    st = [state_ref[:, groups[g]] for g in range(SSM_GROUPS)]
    for c, p in enumerate(prep):
        y_diag, incs, decay = intra[c]
        y_off = jnp.concatenate([_dot(p["cm"][:, gl[g]], st[g].astype(BF16)) for g in range(SSM_GROUPS)],
                                axis=1)
        y = (y_diag + y_off * jnp.exp(a_cs[c]) + dsk_ref[...] * p["xs"]) * p["zg"]
        y_ref[c * CHUNK:(c + 1) * CHUNK, :] = y
        st = [st[g] * decay[:, groups[g]] + incs[g] for g in range(SSM_GROUPS)]
    for g in range(SSM_GROUPS):
        state_ref[:, groups[g]] = st[g]

    y = y_ref[...]
    y = jnp.concatenate([_rms(y[:, groups[g]], NORM_EPS) for g in range(SSM_GROUPS)], axis=1)
    return (y * nw_ref[...]).astype(BF16)


def _prep_block(u_ref, zdt_ref, slabs, mu_ref, w2a_ref, w0_ref, a0_ref, kk_ref, ka_ref, dtb_ref, alog_ref,
                consts):
    W = RWKV_WIDTH
    n = len(slabs)
    part, lats, squares, dts = [], [], [], []
    for rows in slabs:
        lo = 8 + rows.start
        u = u_ref[lo:lo + CHUNK, :]
        prev = u_ref[lo - 1:lo - 1 + CHUNK, :]
        us = u + (prev - u) * mu_ref[...]
        lat = us[:, 4 * W:]
        lats.append(jnp.where(_iota(lat.shape, 1) < LORA, jnp.tanh(lat), lat).astype(BF16))
        kkr = us[:, W:2 * W] * kk_ref[...]
        squares.append((kkr * kkr).astype(BF16))
        dts.append(_softplus(zdt_ref[rows, SSM_WIDTH:] + dtb_ref[...]))
        part.append((us, kkr))
    lora = _dot(jnp.concatenate(lats, axis=0), w2a_ref[...])
    ss = _dot(jnp.concatenate(squares, axis=0), consts["head_ones"])
    prep, logws = [], []
    for i, (us, kkr) in enumerate(part):
        rows = slice(i * CHUNK, (i + 1) * CHUNK)
        w_log = -_softplus(-(w0_ref[...] + lora[rows, :W])) - 0.5
        a = _sigmoid(a0_ref[...] + lora[rows, W:])
        k = us[:, W:2 * W]
        kk = kkr * lax.rsqrt(jnp.maximum(ss[rows], L2_EPS * L2_EPS))
        logws.append(-jnp.exp(w_log))
        prep.append(dict(r=us[:, :W], v=us[:, 2 * W:3 * W], g=us[:, 3 * W:4 * W], logw=logws[-1],
                         kk=kk, k2=k * (1.0 + (a - 1.0) * ka_ref[...]), b=kk * a))
    a_neg = -jnp.exp(alog_ref[...])
    sums = _chunk_cumsum([jnp.concatenate([lw, dt * a_neg], axis=1) for lw, dt in zip(logws, dts)],
                         consts["tril"])
    dt_wide = _dot(jnp.concatenate(dts, axis=0).astype(BF16), consts["head_spread"])
    acs_wide = _dot_split_lhs(jnp.concatenate([s[:, W:] for s in sums], axis=0), consts["head_spread"])
    cut = lambda x: [x[i * CHUNK:(i + 1) * CHUNK] for i in range(n)]
    return prep, [s[:, :W] for s in sums], cut(dt_wide), cut(acs_wide)


def _rwkv_stages(prep, cum, rk_ref, lw_ref, lb_ref, state_ref, y_ref, consts, side_work):
    W = RWKV_WIDTH
    blk, head_ones, head_avg = consts["head_blk"], consts["head_ones"], consts["head_avg"]
    t_i = _iota((CHUNK, W), 0)
    j_i = _iota((CHUNK, W), 1) & (CHUNK - 1)
    one_zero = lambda m: jnp.where(m, 1.0, 0.0).astype(BF16)
    strict = one_zero(j_i < t_i)
    incl = one_zero(j_i <= t_i)
    head_masks_rows = [one_zero((_iota((CHUNK, W), 1) >> 6) == h) for h in range(RWKV_HEADS)]
    eye_pair = jnp.where((_iota((CHUNK, 128), 1) & (CHUNK - 1)) == _iota((CHUNK, 128), 0), 1.0, 0.0)
    first_of_pair = jnp.where((_iota((CHUNK, W), 1) & 127) < CHUNK, 1.0, 0.0).astype(BF16)
    second_of_pair = 1.0 - first_of_pair
    zero_blk = jnp.zeros((CHUNK, W), BF16)

    def bd(x):
        return jnp.concatenate([x] * RWKV_HEADS, axis=0) * head_ones

    ats, rts, vbs, a_ak, a_rk, a_rb, p_pairs, t_pairs = [], [], [], [], [], [], [], []
    for p, cum_c in zip(prep, cum):
        e_n = jnp.exp(-cum_c)
        at = (-p["kk"] * jnp.exp(cum_c - p["logw"])).astype(BF16)
        rt = p["r"] * jnp.exp(cum_c)
        kt = (p["k2"] * e_n).astype(BF16)
        bt = (p["b"] * e_n).astype(BF16)
        ats.append(at)
        rts.append(rt)
        vbs.append(p["v"].astype(BF16))
        x_cat = jnp.concatenate([at, rt.astype(BF16)], axis=0)
        y_stack = jnp.concatenate([y * m for y in (kt, bt) for m in head_masks_rows], axis=0)
        aa = _dot_nt(x_cat, y_stack).astype(BF16)
        a_ak.append(aa[:CHUNK, :W] * strict)
        a_ab = aa[:CHUNK, W:] * strict
        a_rk.append(aa[CHUNK:, :W] * incl)
        a_rb.append(aa[CHUNK:, W:] * incl)
        p_pairs.append([a_ab[:, :128], a_ab[:, 128:]])
        t_pairs.append([eye_pair, eye_pair])

    for step in range(6):
        for c in range(len(p_pairs)):
            for h2 in range(RWKV_HEADS // 2):
                pb = p_pairs[c][h2]
                pt = jnp.concatenate([pb, t_pairs[c][h2].astype(BF16)], axis=1)
                rhs = jnp.concatenate([pt * first_of_pair, pt * second_of_pair], axis=0)
                res = _dot(pb, rhs)
                p_pairs[c][h2] = res[:, :128].astype(BF16)
                t_pairs[c][h2] = t_pairs[c][h2] + res[:, 128:]
        side_work(step)

    nc = len(prep)
    v_bds = [bd(vbs[c]) for c in range(nc)]
    avs = [_dot(a_ak[c], v_bds[c]).astype(BF16) for c in range(nc)]
    wus = [_dot(jnp.concatenate(t_pairs[c], axis=1).astype(BF16),
                jnp.concatenate([bd(ats[c]), bd(avs[c])], axis=1)).astype(BF16) for c in range(nc)]
    qs = [(rts[c] + _dot(a_rb[c], bd(wus[c][:, :W]))).astype(BF16) for c in range(nc)]
    y0s = [_dot(jnp.concatenate([a_rk[c], a_rb[c]], axis=1),
                jnp.concatenate([v_bds[c], bd(wus[c][:, W:])], axis=0)) for c in range(nc)]
    ps, ns, decays = [], [], []
    for c, (p, cum_c) in enumerate(zip(prep, cum)):
        cum_last = cum_c[CHUNK - 1:CHUNK, :]
        e_end = jnp.exp(cum_last - cum_c)
        decays.append(jnp.exp(cum_last))
        mn_lhs = jnp.concatenate([wus[c], jnp.concatenate([zero_blk, vbs[c]], axis=1)], axis=0)
        bk = jnp.concatenate([p["b"] * e_end, p["k2"] * e_end], axis=0).astype(BF16)
        mn = _dot_tn(mn_lhs, bk)
        ps.append(jnp.where(blk, mn[:W], 0.0).astype(BF16))
        ns.append(jnp.where(blk, mn[W:], 0.0))

    s_bd = state_ref[...]
    for c in range(len(prep)):
        sb = s_bd.astype(BF16)
        y_ref[c * CHUNK:(c + 1) * CHUNK, :] = _dot_nt(qs[c], sb) + y0s[c]
        s_bd = s_bd * decays[c] + _dot(sb, ps[c]) + ns[c]
    state_ref[...] = s_bd

    whole = lambda key: jnp.concatenate([p[key] for p in prep], axis=0)
    r, k2, v, g = whole("r"), whole("k2"), whole("v"), whole("g")
    y = y_ref[...]
    yc = y - _dot(y.astype(BF16), head_avg)
    var = _dot((yc * yc).astype(BF16), head_avg)
    yn = yc * lax.rsqrt(var + LNX_EPS) * lw_ref[...] + lb_ref[...]
    bonus = _dot((r * k2 * rk_ref[...]).astype(BF16), head_ones) * v
    return ((yn + bonus) * _silu(g)).astype(BF16)


def _xattn_block(u, kt_ref, vb_ref):
    s = _dot(u[:, :XATTN_WIDTH].astype(BF16), kt_ref[...])
    head = _iota((u.shape[0], XATTN_WIDTH), 1) >> 6
    ps, inv = [], None
    for h in range(XATTN_HEADS):
        sh = s[:, h * MEM_LEN:(h + 1) * MEM_LEN]
        ph = jnp.exp(sh - jnp.max(sh, axis=-1, keepdims=True))
        ps.append(ph)
        r = 1.0 / jnp.sum(ph, axis=-1, keepdims=True)
        inv = r if inv is None else jnp.where(head >= h, r, inv)
    o = _dot(jnp.concatenate(ps, axis=1).astype(BF16), vb_ref[...]) * inv
    return (o * _silu(u[:, XATTN_WIDTH:])).astype(BF16)


def _layer_kernel(x_ref, prew_ref, wmain_ref, wrest_ref,
                  cw_ref, cb_ref, dtb_ref, alog_ref, dsk_ref, snw_ref,
                  mu_ref, w2a_ref, w0_ref, a0_ref, kk_ref, ka_ref, rk_ref, lw_ref, lb_ref,
                  kt_ref, vb_ref, wout_ref, postw_ref,
                  o_ref,
                  rw_u, ssd_pre, zdt, xa, ssd_state, ssd_y, rw_state, rw_y):
    tb = x_ref.shape[0]

    @pl.when(pl.program_id(0) == 0)
    def _():
        rw_u[0:8, :] = jnp.zeros((8, RWKV_IN), F32)
        ssd_pre[0:8, :] = jnp.zeros((8, CONV_DIM), F32)
        ssd_state[...] = jnp.zeros_like(ssd_state)
        rw_state[...] = jnp.zeros_like(rw_state)

    consts = _constants()
    hb = (_rms(x_ref[...], NORM_EPS) * prew_ref[...]).astype(BF16)
    u_rw_dt = _dot(hb, wrest_ref[:, :RWKV_IN + 128])
    rw_u[8:8 + tb, :] = u_rw_dt[:, :RWKV_IN]
    zdt[:, SSM_WIDTH:] = u_rw_dt[:, RWKV_IN:]
    ssd_pre[8:8 + tb, :] = _dot(hb, wmain_ref[:, :CONV_DIM])
    rw_prep, sd_prep, cum, a_cs, dt_wide = [], [], [], [], []
    chunks = _chunks(tb)
    per_block = CUMSUM_BLOCK // CHUNK
    for i in range(0, len(chunks), per_block):
        block = _prep_block(rw_u, zdt, chunks[i:i + per_block], mu_ref, w2a_ref, w0_ref, a0_ref, kk_ref,
                            ka_ref, dtb_ref, alog_ref, consts)
        for acc, new in zip((rw_prep, cum, dt_wide, a_cs), block):
            acc.extend(new)
    zdt[:, :SSM_WIDTH] = _dot(hb, wmain_ref[:, CONV_DIM:])
    xa[...] = _dot(hb, wrest_ref[:, RWKV_IN + 128:])
    y_mem = _xattn_block(xa[...], kt_ref, vb_ref)
    _keep_tail(rw_u)

    intra, queue = [], []

    def ssd_prep_slab(c):
        sd_prep.append(_ssd_prep(ssd_pre, zdt, chunks[c], cw_ref, cb_ref, dt_wide[c]))

    for c in range(len(chunks)):
        queue.append(functools.partial(ssd_prep_slab, c))
    for c in range(tb // CHUNK):
        queue.append(lambda c=c: intra.append(_ssd_intra(sd_prep[c], a_cs[c], consts)))
    total = len(queue)

    def ssd_side_work(step, steps=6):
        while len(queue) > total - -(-(step + 1) * total // steps):
            queue.pop(0)()

    y_rw = _rwkv_stages(rw_prep, cum, rk_ref, lw_ref, lb_ref, rw_state, rw_y, consts, ssd_side_work)
    assert not queue and len(intra) == len(sd_prep)
    _keep_tail(ssd_pre)
    y_ssm = _ssd_state_stage(sd_prep, a_cs, intra, dsk_ref, snw_ref, ssd_state, ssd_y)
    d = _dot(jnp.concatenate([y_ssm, y_rw, y_mem], axis=1), wout_ref[...])
    o_ref[...] = x_ref[...] + _rms(d, NORM_EPS) * postw_ref[...]


def _layer(x2d, params):
    t = x2d.shape[0]
    tb = min(TB, t)
    assert t % tb == 0 and tb % CHUNK == 0
    rows = lambda i: (i, 0)
    whole = lambda a: pl.BlockSpec(a.shape, lambda i: (0,) * a.ndim)
    return pl.pallas_call(
        _layer_kernel,
        grid=(t // tb,),
        in_specs=[pl.BlockSpec((tb, D_MODEL), rows)] + [whole(p) for p in params],
        out_specs=pl.BlockSpec((tb, D_MODEL), rows),
        out_shape=jax.ShapeDtypeStruct((t, D_MODEL), F32),
        scratch_shapes=[pltpu.VMEM((tb + 8, RWKV_IN), F32),
                        pltpu.VMEM((tb + 8, CONV_DIM), F32),
                        pltpu.VMEM((tb, SSM_WIDTH + 128), F32),
                        pltpu.VMEM((tb, XATTN_IN), F32),
                        pltpu.VMEM((SSM_STATE, SSM_WIDTH), F32),
                        pltpu.VMEM((tb, SSM_WIDTH), F32),
                        pltpu.VMEM((RWKV_WIDTH, RWKV_WIDTH), F32),
                        pltpu.VMEM((tb, RWKV_WIDTH), F32)],
        compiler_params=pltpu.CompilerParams(dimension_semantics=("arbitrary",),
                                             vmem_limit_bytes=VMEM_LIMIT),
        name="layer",
    )(x2d, *params)


def _block_diag_heads(blocks):
    h, r, c = blocks.shape
    eye = jnp.eye(h, dtype=blocks.dtype)
    return (eye[:, None, :, None] * blocks[:, :, None, :]).reshape(h * r, h * c)


def kernel(x, mem, mem_norm_w, w_mem_kv, pre_norm_w, w_in, conv_w, conv_b, dt_bias, a_log, d_skip,
           ssm_norm_w, shift_mu, w0, w2, a0, a2, k_k, k_a, r_k, lnx_w, lnx_b, w_out, post_norm_w):
    assert x.shape[0] == 1 and mem.shape[0] == 1
    rep = lambda p: jnp.repeat(p, HEAD_DIM, axis=-1)
    row = lambda p: p.reshape(1, -1)

    kv = _memkv(mem[0], row(mem_norm_w), w_mem_kv.astype(BF16))
    mk = kv[:, :XATTN_WIDTH].reshape(MEM_LEN, XATTN_HEADS, HEAD_DIM)
    mv = kv[:, XATTN_WIDTH:].reshape(MEM_LEN, XATTN_HEADS, HEAD_DIM)
    kt_bd = _block_diag_heads(jnp.transpose(mk, (1, 2, 0)) * (HEAD_DIM ** -0.5)).astype(BF16)
    v_bd = _block_diag_heads(jnp.transpose(mv, (1, 0, 2))).astype(BF16)

    pad8 = lambda p: jnp.pad(p, [(0, 0)] * (p.ndim - 1) + [(0, 128 - SSM_HEADS)])
    w_main, w_rest = _wprep(w_in)
    w_out_k = w_out.astype(BF16)
    zeros = jnp.zeros((DEPTH, LORA, RWKV_WIDTH), F32)
    w2a = jnp.concatenate([jnp.concatenate([w2, zeros], axis=-1),
                           jnp.concatenate([zeros, a2], axis=-1)], axis=1).astype(BF16)

    xc = x[0]
    for i in range(DEPTH):
        params = (row(pre_norm_w[i]), w_main[i], w_rest[i],
                  conv_w[i], row(conv_b[i]), row(pad8(dt_bias[i])), row(pad8(a_log[i])),
                  row(rep(d_skip[i])), row(ssm_norm_w[i]),
                  row(shift_mu[i]), w2a[i], row(w0[i]), row(a0[i]), row(k_k[i]), row(k_a[i]),
                  row(r_k[i]), row(lnx_w[i]), row(lnx_b[i]),
                  kt_bd, v_bd, w_out_k[i], row(post_norm_w[i]))
        xc = _layer(xc, params)
    return xc[None]
```
